```python
import jax, jax.numpy as jnp
from jax import lax
import numpy as np

D_MODEL = 2048
BATCH = 16
SEQ = 2048
DEPTH = 1
DEC_BATCH = 16
DEC_SEQ = 64
PAST_LEN = 2048

CHUNK = 64
Q_BLOCK = 128
N_HEADS_A = 8
HEAD_DIM_A = 128
ATTN_WIDTH = N_HEADS_A * HEAD_DIM_A
POOL_WINDOWS = (2, 4, 8, 16)
N_POOL_GROUPS = len(POOL_WINDOWS)
POOL_WIDTH = D_MODEL // 2
POOL_GROUP = POOL_WIDTH // N_POOL_GROUPS
POOL_OUT_GROUP = D_MODEL // N_POOL_GROUPS
POOL_HIST = max(POOL_WINDOWS) - 1
D_FF = 5632
RMS_EPS = 1e-6
NEG_INF = -1e30
IN_WIDTH = 3 * ATTN_WIDTH + N_HEADS_A + POOL_WIDTH + 2 * D_MODEL
IN_SPLITS = (ATTN_WIDTH, 2 * ATTN_WIDTH, 3 * ATTN_WIDTH, 3 * ATTN_WIDTH + N_HEADS_A,
             3 * ATTN_WIDTH + N_HEADS_A + POOL_WIDTH, 3 * ATTN_WIDTH + N_HEADS_A + POOL_WIDTH + D_MODEL)

kernel_name = "fox_pool_macaron_streaming_step"


def _rmsnorm(x, g):
    xf = x.astype(jnp.float32)
    r = lax.rsqrt(jnp.mean(xf * xf, axis=-1, keepdims=True) + RMS_EPS)
    return (xf * r).astype(x.dtype) * g


def _half_ffn(x, g, w_gate, w_up, w_down):
    h = _rmsnorm(x, g)
    return (jax.nn.silu(h @ w_gate) * (h @ w_up)) @ w_down


def _fox_block(q, cq, pq, k, v, ck, pk):
    s = jnp.einsum('bqhd,bkhd->bhqk', q, k).astype(jnp.float32) * (HEAD_DIM_A ** -0.5)
    bias = jnp.swapaxes(cq, 1, 2)[:, :, :, None] - jnp.swapaxes(ck, 1, 2)[:, :, None, :]
    mask = pk[None, :] <= pq[:, None]
    p = jax.nn.softmax(jnp.where(mask, s + bias, NEG_INF), axis=-1)
    return jnp.einsum('bhqk,bkhd->bqhd', p.astype(v.dtype), v)


def _fox_prompt(q, k, v, c):
    B, S, H, Dh = q.shape
    nb = S // Q_BLOCK
    pos = jnp.arange(S)
    qb = jnp.swapaxes(q.reshape(B, nb, Q_BLOCK, H, Dh), 0, 1)
    cb = jnp.swapaxes(c.reshape(B, nb, Q_BLOCK, H), 0, 1)
    pb = pos.reshape(nb, Q_BLOCK)
    ob = lax.map(lambda a: _fox_block(a[0], a[1], a[2], k, v, c, pos), (qb, cb, pb))
    return jnp.swapaxes(ob, 0, 1).reshape(B, S, H * Dh)


def _fox_sample(q, k, v, logf, k_past, v_past, logf_past):
    B, T, H, Dh = q.shape
    P = k_past.shape[1]
    k_all = jnp.concatenate([k_past, k], axis=1)
    v_all = jnp.concatenate([v_past, v], axis=1)
    c_all = jnp.cumsum(jnp.concatenate([logf_past.astype(jnp.float32), logf], axis=1), axis=1)
    pos_all = jnp.arange(P + T)
    o = _fox_block(q, c_all[:, P:], P + jnp.arange(T), k_all, v_all, c_all, pos_all)
    return o.reshape(B, T, H * Dh)


def _pool_branch(u, hist, pos0, w_pool_group, pool_scale):
    B, T, C = u.shape
    padded = jnp.concatenate([hist, u], axis=1).astype(jnp.float32)
    cs = jnp.concatenate([jnp.zeros((B, 1, C), jnp.float32), jnp.cumsum(padded, axis=1)], axis=1)
    pos = pos0 + jnp.arange(T)
    uf = u.astype(jnp.float32)
    outs = []
    for g, w in enumerate(POOL_WINDOWS):
        sl = slice(g * POOL_GROUP, (g + 1) * POOL_GROUP)
        csg = cs[:, :, sl]
        win_sum = csg[:, POOL_HIST + 1:POOL_HIST + 1 + T] - csg[:, POOL_HIST + 1 - w:POOL_HIST + 1 - w + T]
        cnt = jnp.minimum(w, pos + 1).astype(jnp.float32)[None, :, None]
        outs.append(win_sum / cnt - uf[:, :, sl])
    d = jnp.stack(outs, axis=2).astype(u.dtype)
    y = jnp.einsum('btgc,gcd->btgd', d, w_pool_group).reshape(B, T, D_MODEL)
    return y * pool_scale


def _layer(x, pool_hist, pos0, past, ffn1_norm, ffn1_w_gate, ffn1_w_up, ffn1_w_down, mix_norm, w_in, b_forget,
           w_branch_attn, w_pool_group, pool_scale, w_out, ffn2_norm, ffn2_w_gate, ffn2_w_up, ffn2_w_down):
    B, T, _ = x.shape
    x = x + 0.5 * _half_ffn(x, ffn1_norm, ffn1_w_gate, ffn1_w_up, ffn1_w_down)
    h = _rmsnorm(x, mix_norm)
    z = h @ w_in
    q, k, v, f_lin, u, g_a, g_b = jnp.split(z, IN_SPLITS, axis=-1)
    q = q.reshape(B, T, N_HEADS_A, HEAD_DIM_A)
    k = k.reshape(B, T, N_HEADS_A, HEAD_DIM_A)
    v = v.reshape(B, T, N_HEADS_A, HEAD_DIM_A)
    logf = jax.nn.log_sigmoid(f_lin.astype(jnp.float32) + b_forget.astype(jnp.float32))
    if past is None:
        o = _fox_prompt(q, k, v, jnp.cumsum(logf, axis=1))
    else:
        o = _fox_sample(q, k, v, logf, past[0], past[1], past[2])
    attn_out = o @ w_branch_attn
    pool_out = _pool_branch(u, pool_hist, pos0, w_pool_group, pool_scale)
    mixed = jax.nn.sigmoid(g_a) * attn_out + jax.nn.sigmoid(g_b) * pool_out
    x = x + mixed @ w_out
    x = x + 0.5 * _half_ffn(x, ffn2_norm, ffn2_w_gate, ffn2_w_up, ffn2_w_down)
    new_pool = jnp.concatenate([pool_hist, u], axis=1)[:, -POOL_HIST:]
    return x, k, v, logf, new_pool


def setup_inputs(seed: int = 0) -> dict:
    key = jax.random.key(seed)
    ks = jax.random.split(key, 24)
    f32 = jnp.float32

    def nrm(k, shape, scale):
        return jax.random.normal(k, shape, f32) * scale

    def gain(k, shape):
        return 1.0 + 0.02 * jax.random.normal(k, shape, f32)

    L = DEPTH
    return {
        "x_prompt": nrm(ks[0], (BATCH, SEQ, D_MODEL), 1.0),
        "x_sample": nrm(ks[1], (DEC_BATCH, DEC_SEQ, D_MODEL), 1.0),
        "cache_k": nrm(ks[2], (L, DEC_BATCH, PAST_LEN, N_HEADS_A, HEAD_DIM_A), 1.0),
        "cache_v": nrm(ks[3], (L, DEC_BATCH, PAST_LEN, N_HEADS_A, HEAD_DIM_A), 1.0),
        "cache_logf": jax.nn.log_sigmoid(3.0 + jax.random.normal(ks[4], (L, DEC_BATCH, PAST_LEN, N_HEADS_A), f32)),
        "state_pool": nrm(ks[5], (L, DEC_BATCH, POOL_HIST, POOL_WIDTH), 1.0),
        "ffn1_norm": gain(ks[6], (L, D_MODEL)),
        "ffn1_w_gate": nrm(ks[7], (L, D_MODEL, D_FF), D_MODEL ** -0.5),
        "ffn1_w_up": nrm(ks[8], (L, D_MODEL, D_FF), D_MODEL ** -0.5),
        "ffn1_w_down": nrm(ks[9], (L, D_FF, D_MODEL), D_FF ** -0.5),
        "mix_norm": gain(ks[10], (L, D_MODEL)),
        "w_in": nrm(ks[11], (L, D_MODEL, IN_WIDTH), D_MODEL ** -0.5),
        "b_forget": 3.0 + 0.1 * jax.random.normal(ks[12], (L, N_HEADS_A), f32),
        "w_branch_attn": nrm(ks[13], (L, ATTN_WIDTH, D_MODEL), ATTN_WIDTH ** -0.5),
        "w_pool_group": nrm(ks[14], (L, N_POOL_GROUPS, POOL_GROUP, POOL_OUT_GROUP), POOL_GROUP ** -0.5),
        "pool_scale": gain(ks[15], (L, D_MODEL)),
        "w_out": nrm(ks[16], (L, D_MODEL, D_MODEL), D_MODEL ** -0.5),
        "ffn2_norm": gain(ks[17], (L, D_MODEL)),
        "ffn2_w_gate": nrm(ks[18], (L, D_MODEL, D_FF), D_MODEL ** -0.5),
        "ffn2_w_up": nrm(ks[19], (L, D_MODEL, D_FF), D_MODEL ** -0.5),
        "ffn2_w_down": nrm(ks[20], (L, D_FF, D_MODEL), D_FF ** -0.5),
        "final_norm": gain(ks[21], (D_MODEL,)),
    }


def reference(x_prompt, x_sample, cache_k, cache_v, cache_logf, state_pool, ffn1_norm, ffn1_w_gate, ffn1_w_up,
              ffn1_w_down, mix_norm, w_in, b_forget, w_branch_attn, w_pool_group, pool_scale, w_out, ffn2_norm,
              ffn2_w_gate, ffn2_w_up, ffn2_w_down, final_norm):
    hp, hs = x_prompt, x_sample
    kp, vp, lp, pp, ksm, vsm, lsm, psm = [], [], [], [], [], [], [], []
    past_len = cache_k.shape[2]
    for l in range(DEPTH):
        w = (ffn1_norm[l], ffn1_w_gate[l], ffn1_w_up[l], ffn1_w_down[l], mix_norm[l], w_in[l], b_forget[l],
             w_branch_attn[l], w_pool_group[l], pool_scale[l], w_out[l], ffn2_norm[l], ffn2_w_gate[l],
             ffn2_w_up[l], ffn2_w_down[l])
        zero_hist = jnp.zeros((hp.shape[0], POOL_HIST, POOL_WIDTH), hp.dtype)
        hp, k1, v1, l1, p1 = _layer(hp, zero_hist, 0, None, *w)
        hs, k2, v2, l2, p2 = _layer(hs, state_pool[l], past_len, (cache_k[l], cache_v[l], cache_logf[l]), *w)
        kp.append(k1); vp.append(v1); lp.append(l1); pp.append(p1)
        ksm.append(k2); vsm.append(v2); lsm.append(l2); psm.append(p2)
    y_prompt = _rmsnorm(hp, final_norm)
    y_sample = _rmsnorm(hs, final_norm)
    k_prompt = jnp.stack(kp)
    v_prompt = jnp.stack(vp)
    logf_prompt = jnp.stack(lp)
    pool_prompt = jnp.stack(pp)
    k_sample = jnp.stack(ksm)
    v_sample = jnp.stack(vsm)
    logf_sample = jnp.stack(lsm)
    pool_sample = jnp.stack(psm)
    return (y_prompt, y_sample, k_prompt, v_prompt, logf_prompt, pool_prompt, k_sample, v_sample, logf_sample, pool_sample)
```

```python
import functools

import jax
import jax.numpy as jnp
from jax import lax
from jax.experimental import pallas as pl
from jax.experimental.pallas import tpu as pltpu

D_MODEL = 2048
N_HEADS = 8
HEAD_DIM = 128
ATTN_WIDTH = N_HEADS * HEAD_DIM
POOL_WINDOWS = (2, 4, 8, 16)
POOL_WIDTH = D_MODEL // 2
POOL_GROUP = POOL_WIDTH // len(POOL_WINDOWS)
POOL_OUT_GROUP = D_MODEL // len(POOL_WINDOWS)
POOL_HIST = max(POOL_WINDOWS) - 1
RMS_EPS = 1e-6
NEG_INF = -1e30

LANES = 128
HIST_ROWS = 16
VMEM_LIMIT = 56 * 1024 * 1024

F32 = jnp.float32
BF16 = jnp.bfloat16


def _params(semantics):
    return pltpu.CompilerParams(dimension_semantics=semantics, vmem_limit_bytes=VMEM_LIMIT)


def _rmsnorm(x, g):
    r = lax.rsqrt(jnp.mean(x * x, axis=-1, keepdims=True) + RMS_EPS)
    return (x * r) * g


def _resident(shape):
    return pl.BlockSpec(shape, lambda *_: (0,) * len(shape), pipeline_mode=pl.Buffered(1))


def _ffn_kernel(x_ref, g_ref, wg_ref, wu_ref, wd_ref, fn_ref, o_ref, h_ref, *, final_norm):
    j = pl.program_id(1)

    @pl.when(j == 0)
    def _():
        h_ref[...] = _rmsnorm(x_ref[...], g_ref[...]).astype(BF16)
        o_ref[...] = jnp.zeros_like(o_ref)

    h = h_ref[...]
    gate = jnp.dot(h, wg_ref[...], preferred_element_type=F32)
    up = jnp.dot(h, wu_ref[...], preferred_element_type=F32)
    act = (gate * jax.nn.sigmoid(gate) * up).astype(BF16)
    o_ref[...] += jnp.dot(act, wd_ref[...], preferred_element_type=F32)

    @pl.when(j == pl.num_programs(1) - 1)
    def _():
        y = x_ref[...] + 0.5 * o_ref[...]
        if final_norm:
            y = _rmsnorm(y, fn_ref[...])
        o_ref[...] = y


def _ffn(x, g, wg, wu, wd, fn, *, final_norm, tm=512, tf=512):
    t, d = x.shape
    f = wg.shape[1]
    return pl.pallas_call(
        functools.partial(_ffn_kernel, final_norm=final_norm),
        grid=(t // tm, f // tf),
        in_specs=[
            pl.BlockSpec((tm, d), lambda i, j: (i, 0)),
            pl.BlockSpec((1, d), lambda i, j: (0, 0)),
            pl.BlockSpec((d, tf), lambda i, j: (0, j)),
            pl.BlockSpec((d, tf), lambda i, j: (0, j)),
            pl.BlockSpec((tf, d), lambda i, j: (j, 0)),
            pl.BlockSpec((1, d), lambda i, j: (0, 0)),
        ],
        out_specs=pl.BlockSpec((tm, d), lambda i, j: (i, 0)),
        out_shape=jax.ShapeDtypeStruct((t, d), F32),
        scratch_shapes=[pltpu.VMEM((tm, d), BF16)],
        compiler_params=_params(("parallel", "arbitrary")),
        name="ffn",
    )(x, g, wg, wu, wd, fn)


def _log_sigmoid(x):
    return jnp.minimum(x, 0.0) - jnp.log1p(jnp.exp(-jnp.abs(x)))


def _inproj_kernel(x_ref, g_ref, wq_ref, wk_ref, wv_ref, wu_ref, wf_ref, bf_ref,
                   qb_ref, k_ref, v_ref, kb_ref, vb_ref, u_ref, logf_ref):
    h = _rmsnorm(x_ref[...], g_ref[...]).astype(BF16)
    qb_ref[...] = jnp.dot(h, wq_ref[...], preferred_element_type=F32).astype(BF16)
    k = jnp.dot(h, wk_ref[...], preferred_element_type=F32)
    k_ref[...] = k
    kb_ref[...] = k.astype(BF16)
    v = jnp.dot(h, wv_ref[...], preferred_element_type=F32)
    v_ref[...] = v
    vb_ref[...] = v.astype(BF16)
    u_ref[...] = jnp.dot(h, wu_ref[...], preferred_element_type=F32)
    f = jnp.dot(h, wf_ref[...], preferred_element_type=F32)
    logf_ref[...] = _log_sigmoid(f[:, :N_HEADS] + bf_ref[...])


def _inproj(x, g, wq, wk, wv, wu, wf, bforget, *, tm=256):
    t, d = x.shape
    row = lambda width: pl.BlockSpec((tm, width), lambda i: (i, 0))
    return pl.pallas_call(
        _inproj_kernel,
        grid=(t // tm,),
        in_specs=[
            row(d),
            _resident((1, d)),
            _resident((d, ATTN_WIDTH)),
            _resident((d, ATTN_WIDTH)),
            _resident((d, ATTN_WIDTH)),
            _resident((d, POOL_WIDTH)),
            _resident((d, LANES)),
            _resident((1, N_HEADS)),
        ],
        out_specs=[row(ATTN_WIDTH), row(ATTN_WIDTH), row(ATTN_WIDTH), row(ATTN_WIDTH), row(ATTN_WIDTH),
                   row(POOL_WIDTH), row(N_HEADS)],
        out_shape=[
            jax.ShapeDtypeStruct((t, ATTN_WIDTH), BF16),
            jax.ShapeDtypeStruct((t, ATTN_WIDTH), F32),
            jax.ShapeDtypeStruct((t, ATTN_WIDTH), F32),
            jax.ShapeDtypeStruct((t, ATTN_WIDTH), BF16),
            jax.ShapeDtypeStruct((t, ATTN_WIDTH), BF16),
            jax.ShapeDtypeStruct((t, POOL_WIDTH), F32),
            jax.ShapeDtypeStruct((t, N_HEADS), F32),
        ],
        compiler_params=_params(("parallel",)),
        name="inproj",
    )(x, g, wq, wk, wv, wu, wf, bforget)


def _cumsum_kernel(x_ref, o_ref, buf_ref, *, pad):
    n = x_ref.shape[1]
    buf_ref[pl.ds(0, pad), :] = jnp.zeros((pad, x_ref.shape[2]), F32)
    acc = x_ref[0]
    shift = 1
    while shift < n:
        buf_ref[pl.ds(pad, n), :] = acc
        acc = acc + buf_ref[pl.ds(pad - shift, n), :]
        shift *= 2
    o_ref[0] = acc


def _cumsum_rows(x):
    b, n, h = x.shape
    pad = 1
    while pad < n:
        pad *= 2
    pad //= 2
    pad = max(pad, 8)
    return pl.pallas_call(
        functools.partial(_cumsum_kernel, pad=pad),
        grid=(b,),
        in_specs=[pl.BlockSpec((1, n, h), lambda i: (i, 0, 0))],
        out_specs=pl.BlockSpec((1, n, h), lambda i: (i, 0, 0)),
        out_shape=jax.ShapeDtypeStruct((b, n, h), F32),
        scratch_shapes=[pltpu.VMEM((pad + n, h), F32)],
        compiler_params=_params(("parallel",)),
        name="cumsum",
    )(x)


def _softmax_step(carry, q, k, v, cq, ck, mask, scale):
    m, l, acc = carry
    s = lax.dot_general(q, k, (((1,), (1,)), ((), ())), preferred_element_type=F32) * scale
    logit = s + (cq - ck)
    if mask is not None:
        logit = jnp.where(mask, logit, NEG_INF)
    m_new = jnp.maximum(m, jnp.max(logit, axis=1, keepdims=True))
    alpha = jnp.exp(m - m_new)
    p = jnp.exp(logit - m_new)
    l = alpha * l + jnp.sum(p, axis=1, keepdims=True)
    acc = alpha * acc + jnp.dot(p.astype(BF16), v, preferred_element_type=F32)
    return m_new, l, acc


def _softmax_init(tq):
    return (jnp.full((tq, 1), NEG_INF, F32), jnp.zeros((tq, 1), F32), jnp.zeros((tq, HEAD_DIM), F32))


def _attn_prompt_kernel(q_ref, k_ref, v_ref, c_ref, ct_ref, o_ref, *, tq, nblk, scale):
    qi = pl.program_id(1)
    q0 = pl.multiple_of(qi * tq, tq)
    row = lax.broadcasted_iota(jnp.int32, (tq, tq), 0)
    col = lax.broadcasted_iota(jnp.int32, (tq, tq), 1)
    diag_mask = col <= row
    for h in range(N_HEADS):
        hs = slice(h * HEAD_DIM, (h + 1) * HEAD_DIM)
        q = q_ref[:, hs]
        cq = c_ref[0, pl.ds(q0, tq), h:h + 1]

        def block(kb, carry, mask, h=h, hs=hs, q=q, cq=cq):
            k0 = pl.multiple_of(kb * tq, tq)
            k = k_ref[pl.ds(k0, tq), hs]
            v = v_ref[pl.ds(k0, tq), hs]
            ck = ct_ref[0, pl.ds(h * nblk + kb, 1), :]
            return _softmax_step(carry, q, k, v, cq, ck, mask, scale)

        carry = lax.fori_loop(0, qi, lambda kb, c: block(kb, c, None), _softmax_init(tq))
        _, l, acc = block(qi, carry, diag_mask)
        o_ref[:, hs] = (acc / l).astype(BF16)


def _attn_prompt(qb, kb, vb, c, *, batch, seq, tq=256):
    nblk = seq // tq
    ct = jnp.swapaxes(c, 1, 2).reshape(batch, N_HEADS * nblk, tq)
    return pl.pallas_call(
        functools.partial(_attn_prompt_kernel, tq=tq, nblk=nblk, scale=HEAD_DIM ** -0.5),
        grid=(batch, nblk),
        in_specs=[
            pl.BlockSpec((tq, ATTN_WIDTH), lambda b, i: (b * nblk + i, 0)),
            pl.BlockSpec((seq, ATTN_WIDTH), lambda b, i: (b, 0)),
            pl.BlockSpec((seq, ATTN_WIDTH), lambda b, i: (b, 0)),
            pl.BlockSpec((1, seq, N_HEADS), lambda b, i: (b, 0, 0)),
            pl.BlockSpec((1, N_HEADS * nblk, tq), lambda b, i: (b, 0, 0)),
        ],
        out_specs=pl.BlockSpec((tq, ATTN_WIDTH), lambda b, i: (b * nblk + i, 0)),
        out_shape=jax.ShapeDtypeStruct((batch * seq, ATTN_WIDTH), BF16),
        compiler_params=_params(("parallel", "arbitrary")),
        name="attn_prompt",
    )(qb, kb, vb, c, ct)


def _attn_sample_kernel(q_ref, kp_ref, vp_ref, kn_ref, vn_ref, cq_ref, ctp_ref, ctn_ref, o_ref, *, tq, scale):
    row = lax.broadcasted_iota(jnp.int32, (tq, tq), 0)
    col = lax.broadcasted_iota(jnp.int32, (tq, tq), 1)
    diag_mask = col <= row
    for h in range(N_HEADS):
        hs = slice(h * HEAD_DIM, (h + 1) * HEAD_DIM)
        q = q_ref[:, hs]
        cq = cq_ref[0, :, h:h + 1]
        carry = _softmax_init(tq)
        carry = _softmax_step(carry, q, kp_ref[0, :, hs].astype(BF16), vp_ref[0, :, hs].astype(BF16),
                              cq, ctp_ref[0, h:h + 1, :], None, scale)
        _, l, acc = _softmax_step(carry, q, kn_ref[:, hs], vn_ref[:, hs],
                                  cq, ctn_ref[0, h:h + 1, :], diag_mask, scale)
        o_ref[:, hs] = (acc / l).astype(BF16)


def _attn_sample(qb, kb, vb, k_past, v_past, c_all, *, batch, tq, past):
    cq = c_all[:, past:, :]
    ct = jnp.swapaxes(c_all, 1, 2)
    ct_past, ct_new = ct[:, :, :past], ct[:, :, past:]
    new = lambda: pl.BlockSpec((tq, ATTN_WIDTH), lambda b: (b, 0))
    old = lambda: pl.BlockSpec((1, past, ATTN_WIDTH), lambda b: (b, 0, 0))
    return pl.pallas_call(
        functools.partial(_attn_sample_kernel, tq=tq, scale=HEAD_DIM ** -0.5),
        grid=(batch,),
        in_specs=[
            new(), old(), old(), new(), new(),
            pl.BlockSpec((1, tq, N_HEADS), lambda b: (b, 0, 0)),
            pl.BlockSpec((1, N_HEADS, past), lambda b: (b, 0, 0)),
            pl.BlockSpec((1, N_HEADS, tq), lambda b: (b, 0, 0)),
        ],
        out_specs=new(),
        out_shape=jax.ShapeDtypeStruct((batch * tq, ATTN_WIDTH), BF16),
        compiler_params=_params(("parallel",)),
        name="attn_sample",
    )(qb, k_past, v_past, kb, vb, cq, ct_past, ct_new)


def _pool_kernel(u_ref, prev_ref, d_ref, ext_ref, *, ts, pos0, zero_first):
    s = pl.program_id(1)
    prev = prev_ref[0]
    if zero_first:
        prev = jnp.where(s == 0, 0.0, prev)
    cur = u_ref[0]
    ext_ref[pl.ds(0, HIST_ROWS), :] = prev
    ext_ref[pl.ds(HIST_ROWS, ts), :] = cur
    pos = pos0 + s * ts + lax.broadcasted_iota(jnp.int32, (ts, 1), 0)
    for g, w in enumerate(POOL_WINDOWS):
        gs = slice(g * POOL_GROUP, (g + 1) * POOL_GROUP)
        win = cur[:, gs]
        for back in range(1, w):
            win = win + ext_ref[pl.ds(HIST_ROWS - back, ts), gs]
        cnt = jnp.minimum(w, pos + 1).astype(F32)
        d_ref[0, :, gs] = (win / cnt - cur[:, gs]).astype(BF16)


def _pool(u, prev, *, ts, pos0, zero_first, prev_from_u):
    b, n, c = u.shape
    steps = n // ts
    if prev_from_u:
        prev_map = lambda i, s: (i, jnp.maximum(s * (ts // HIST_ROWS) - 1, 0), 0)
    else:
        prev_map = lambda i, s: (i, 0, 0)
    return pl.pallas_call(
        functools.partial(_pool_kernel, ts=ts, pos0=pos0, zero_first=zero_first),
        grid=(b, steps),
        in_specs=[
            pl.BlockSpec((1, ts, c), lambda i, s: (i, s, 0)),
            pl.BlockSpec((1, HIST_ROWS, c), prev_map),
        ],
        out_specs=pl.BlockSpec((1, ts, c), lambda i, s: (i, s, 0)),
        out_shape=jax.ShapeDtypeStruct((b, n, c), BF16),
        scratch_shapes=[pltpu.VMEM((HIST_ROWS + ts, c), F32)],
        compiler_params=_params(("parallel", "arbitrary")),
        name="pool",
    )(u, prev)


def _merge_kernel(x_ref, g_ref, o_ref, d_ref, wga_ref, wgb_ref, wat_ref, wpl_ref, ps_ref, wo_ref,
                  y_ref, h_ref):
    n = pl.program_id(1)

    @pl.when(n == 0)
    def _():
        h_ref[...] = _rmsnorm(x_ref[...], g_ref[...]).astype(BF16)
        y_ref[...] = jnp.zeros_like(y_ref)

    h = h_ref[...]
    gate_a = jax.nn.sigmoid(jnp.dot(h, wga_ref[...], preferred_element_type=F32))
    gate_b = jax.nn.sigmoid(jnp.dot(h, wgb_ref[...], preferred_element_type=F32))
    attn = jnp.dot(o_ref[...], wat_ref[...], preferred_element_type=F32)
    pool = jnp.dot(d_ref[...], wpl_ref[0], preferred_element_type=F32) * ps_ref[...]
    mixed = (gate_a * attn + gate_b * pool).astype(BF16)
    y_ref[...] += jnp.dot(mixed, wo_ref[...], preferred_element_type=F32)

    @pl.when(n == pl.num_programs(1) - 1)
    def _():
        y_ref[...] = x_ref[...] + y_ref[...]


def _merge(x, g, o, d, wga, wgb, wat, wpl, ps, wo, *, tm=512):
    t, dm = x.shape
    tn = POOL_OUT_GROUP
    return pl.pallas_call(
        _merge_kernel,
        grid=(t // tm, dm // tn),
        in_specs=[
            pl.BlockSpec((tm, dm), lambda i, n: (i, 0)),
            pl.BlockSpec((1, dm), lambda i, n: (0, 0)),
            pl.BlockSpec((tm, ATTN_WIDTH), lambda i, n: (i, 0)),
            pl.BlockSpec((tm, POOL_GROUP), lambda i, n: (i, n)),
            pl.BlockSpec((dm, tn), lambda i, n: (0, n)),
            pl.BlockSpec((dm, tn), lambda i, n: (0, n)),
            pl.BlockSpec((ATTN_WIDTH, tn), lambda i, n: (0, n)),
            pl.BlockSpec((1, POOL_GROUP, tn), lambda i, n: (n, 0, 0)),
            pl.BlockSpec((1, tn), lambda i, n: (0, n)),
            pl.BlockSpec((tn, dm), lambda i, n: (n, 0)),
        ],
        out_specs=pl.BlockSpec((tm, dm), lambda i, n: (i, 0)),
        out_shape=jax.ShapeDtypeStruct((t, dm), F32),
        scratch_shapes=[pltpu.VMEM((tm, dm), BF16)],
        compiler_params=_params(("parallel", "arbitrary")),
        name="merge",
    )(x, g, o, d, wga, wgb, wat, wpl, ps, wo)


def _layer_weights(l, ffn1_norm, ffn1_w_gate, ffn1_w_up, ffn1_w_down, mix_norm, w_in, b_forget, w_branch_attn,
                   w_pool_group, pool_scale, w_out, ffn2_norm, ffn2_w_gate, ffn2_w_up, ffn2_w_down):
    a = ATTN_WIDTH
    wi = w_in[l]
    f0, u0 = 3 * a, 3 * a + N_HEADS
    ga0 = u0 + POOL_WIDTH
    gb0 = ga0 + D_MODEL
    wf = jnp.pad(wi[:, f0:u0], ((0, 0), (0, LANES - N_HEADS)))
    return dict(
        n1=ffn1_norm[l][None], g1=ffn1_w_gate[l].astype(BF16), u1=ffn1_w_up[l].astype(BF16),
        d1=ffn1_w_down[l].astype(BF16),
        nm=mix_norm[l][None],
        wq=wi[:, :a].astype(BF16), wk=wi[:, a:2 * a].astype(BF16), wv=wi[:, 2 * a:3 * a].astype(BF16),
        wf=wf.astype(BF16), wu=wi[:, u0:ga0].astype(BF16),
        wga=wi[:, ga0:gb0].astype(BF16), wgb=wi[:, gb0:].astype(BF16),
        bf=b_forget[l][None],
        wat=w_branch_attn[l].astype(BF16), wpl=w_pool_group[l].astype(BF16), ps=pool_scale[l][None],
        wo=w_out[l].astype(BF16),
        n2=ffn2_norm[l][None], g2=ffn2_w_gate[l].astype(BF16), u2=ffn2_w_up[l].astype(BF16),
        d2=ffn2_w_down[l].astype(BF16),
    )


def _layer(x, w, final_norm, *, batch, seq, past, last):
    x1 = _ffn(x, w["n1"], w["g1"], w["u1"], w["d1"], w["n1"], final_norm=False)
    qb, k, v, kb, vb, u, logf = _inproj(x1, w["nm"], w["wq"], w["wk"], w["wv"], w["wu"], w["wf"], w["bf"])
    logf3 = logf.reshape(batch, seq, N_HEADS)
    u3 = u.reshape(batch, seq, POOL_WIDTH)
    if past is None:
        c = _cumsum_rows(logf3)
        o = _attn_prompt(qb, kb, vb, c, batch=batch, seq=seq)
        d = _pool(u3, u3, ts=512, pos0=0, zero_first=True, prev_from_u=True)
        new_pool = u3[:, seq - POOL_HIST:, :]
    else:
        k_past, v_past, logf_past, hist = past
        plen = k_past.shape[1]
        c_all = _cumsum_rows(jnp.concatenate([logf_past.astype(F32), logf3], axis=1))
        o = _attn_sample(qb, kb, vb, k_past.reshape(batch, plen, ATTN_WIDTH), v_past.reshape(batch, plen, ATTN_WIDTH),
                         c_all, batch=batch, tq=seq, past=plen)
        hist_rows = jnp.pad(hist, ((0, 0), (HIST_ROWS - POOL_HIST, 0), (0, 0)))
        d = _pool(u3, hist_rows, ts=seq, pos0=plen, zero_first=False, prev_from_u=False)
        new_pool = jnp.concatenate([hist, u3], axis=1)[:, -POOL_HIST:, :]
    x2 = _merge(x1, w["nm"], o, d.reshape(batch * seq, POOL_WIDTH), w["wga"], w["wgb"], w["wat"], w["wpl"],
                w["ps"], w["wo"])
    x3 = _ffn(x2, w["n2"], w["g2"], w["u2"], w["d2"], final_norm, final_norm=last)
    return (x3, k.reshape(batch, seq, N_HEADS, HEAD_DIM), v.reshape(batch, seq, N_HEADS, HEAD_DIM), logf3, new_pool)


def kernel(x_prompt, x_sample, cache_k, cache_v, cache_logf, state_pool, ffn1_norm, ffn1_w_gate, ffn1_w_up,
           ffn1_w_down, mix_norm, w_in, b_forget, w_branch_attn, w_pool_group, pool_scale, w_out, ffn2_norm,
           ffn2_w_gate, ffn2_w_up, ffn2_w_down, final_norm):
    depth = w_in.shape[0]
    bp, sp, _ = x_prompt.shape
    bs, ss, _ = x_sample.shape
    hp = x_prompt.reshape(bp * sp, D_MODEL)
    hs = x_sample.reshape(bs * ss, D_MODEL)
    fn = final_norm[None]
    outs = [[] for _ in range(8)]
    for l in range(depth):
        w = _layer_weights(l, ffn1_norm, ffn1_w_gate, ffn1_w_up, ffn1_w_down, mix_norm, w_in, b_forget,
                           w_branch_attn, w_pool_group, pool_scale, w_out, ffn2_norm, ffn2_w_gate, ffn2_w_up,
                           ffn2_w_down)
        last = l == depth - 1
        hp, k1, v1, l1, p1 = _layer(hp, w, fn, batch=bp, seq=sp, past=None, last=last)
        hs, k2, v2, l2, p2 = _layer(hs, w, fn, batch=bs, seq=ss,
                                    past=(cache_k[l], cache_v[l], cache_logf[l], state_pool[l]), last=last)
        for dst, val in zip(outs, (k1, v1, l1, p1, k2, v2, l2, p2)):
            dst.append(val)
    stacked = [jnp.stack(o) for o in outs]
    return (hp.reshape(bp, sp, D_MODEL), hs.reshape(bs, ss, D_MODEL), *stacked)
```

```python
import functools

import jax
import jax.numpy as jnp
from jax import lax
from jax.experimental import pallas as pl
from jax.experimental.pallas import tpu as pltpu

D_MODEL = 2048
N_HEADS = 8
HEAD_DIM = 128
ATTN_WIDTH = N_HEADS * HEAD_DIM
POOL_WINDOWS = (2, 4, 8, 16)
POOL_WIDTH = D_MODEL // 2
POOL_GROUP = POOL_WIDTH // len(POOL_WINDOWS)
POOL_OUT_GROUP = D_MODEL // len(POOL_WINDOWS)
POOL_HIST = max(POOL_WINDOWS) - 1
RMS_EPS = 1e-6
NEG_INF = -1e30
LOG2E = 1.4426950408889634

LANES = 128
HIST_ROWS = 16
POOL_PAD = 8
ATTN_BLOCK = 256
VMEM_LIMIT = 60 * 1024 * 1024

F32 = jnp.float32
BF16 = jnp.bfloat16


def _params(semantics):
    return pltpu.CompilerParams(dimension_semantics=semantics, vmem_limit_bytes=VMEM_LIMIT)


def _rmsnorm(x, g):
    r = lax.rsqrt(jnp.mean(x * x, axis=-1, keepdims=True) + RMS_EPS)
    return (x * r) * g


def _resident(shape):
    return pl.BlockSpec(shape, lambda *_: (0,) * len(shape), pipeline_mode=pl.Buffered(1))


def _ffn_kernel(x_ref, g_ref, wg_ref, wu_ref, wd_ref, fn_ref, o_ref, h_ref, *, final_norm):
    j = pl.program_id(1)

    @pl.when(j == 0)
    def _():
        x = x_ref[...]
        h_ref[...] = _rmsnorm(x, g_ref[...]).astype(BF16)
        o_ref[...] = x

    h = h_ref[...]
    gate = jnp.dot(h, wg_ref[...], preferred_element_type=F32)
    up = jnp.dot(h, wu_ref[...], preferred_element_type=F32)
    act = (gate * jax.nn.sigmoid(gate) * up).astype(BF16)
    o_ref[...] += jnp.dot(act, wd_ref[...], preferred_element_type=F32)

    if final_norm:
        @pl.when(j == pl.num_programs(1) - 1)
        def _():
            o_ref[...] = _rmsnorm(o_ref[...], fn_ref[...])


def _ffn(x, g, wg, wu, wd, fn, *, final_norm, tm=1024, tf=512):
    t, d = x.shape
    f = wg.shape[1]
    return pl.pallas_call(
        functools.partial(_ffn_kernel, final_norm=final_norm),
        grid=(t // tm, f // tf),
        in_specs=[
            pl.BlockSpec((tm, d), lambda i, j: (i, 0)),
            pl.BlockSpec((1, d), lambda i, j: (0, 0)),
            pl.BlockSpec((d, tf), lambda i, j: (0, j)),
            pl.BlockSpec((d, tf), lambda i, j: (0, j)),
            pl.BlockSpec((tf, d), lambda i, j: (j, 0)),
            pl.BlockSpec((1, d), lambda i, j: (0, 0)),
        ],
        out_specs=pl.BlockSpec((tm, d), lambda i, j: (i, 0)),
        out_shape=jax.ShapeDtypeStruct((t, d), F32),
        scratch_shapes=[pltpu.VMEM((tm, d), BF16)],
        compiler_params=_params(("parallel", "arbitrary")),
        name="ffn",
    )(x, g, wg, wu, wd, fn)


def _log_sigmoid(x):
    return jnp.minimum(x, 0.0) - jnp.log1p(jnp.exp(-jnp.abs(x)))


def _inproj_kernel(x_ref, g_ref, wq_ref, wk_ref, wv_ref, wu_ref, wf_ref, bf_ref,
                   qb_ref, k_ref, v_ref, kb_ref, vb_ref, u_ref, logf_ref, *, transpose_v):
    h = _rmsnorm(x_ref[...], g_ref[...]).astype(BF16)
    qb_ref[...] = jnp.dot(h, wq_ref[...], preferred_element_type=F32).astype(BF16)
    k = jnp.dot(h, wk_ref[...], preferred_element_type=F32)
    k_ref[...] = k
    kb_ref[...] = k.astype(BF16)
    v = jnp.dot(h, wv_ref[...], preferred_element_type=F32)
    v_ref[...] = v
    if transpose_v:
        vb_ref[0] = v.T.astype(BF16)
    else:
        vb_ref[...] = v.astype(BF16)
    u_ref[...] = jnp.dot(h, wu_ref[...], preferred_element_type=F32)
    f = jnp.dot(h, wf_ref[...], preferred_element_type=F32)
    logf_ref[...] = _log_sigmoid(f[:, :N_HEADS] + bf_ref[...])


def _inproj(x, g, wq, wk, wv, wu, wf, bforget, *, transpose_v, tm=ATTN_BLOCK):
    t, d = x.shape
    row = lambda width: pl.BlockSpec((tm, width), lambda i: (i, 0))
    if transpose_v:
        vb_spec = pl.BlockSpec((1, ATTN_WIDTH, tm), lambda i: (i, 0, 0))
        vb_shape = jax.ShapeDtypeStruct((t // tm, ATTN_WIDTH, tm), BF16)
    else:
        vb_spec, vb_shape = row(ATTN_WIDTH), jax.ShapeDtypeStruct((t, ATTN_WIDTH), BF16)
    return pl.pallas_call(
        functools.partial(_inproj_kernel, transpose_v=transpose_v),
        grid=(t // tm,),
        in_specs=[
            row(d),
            _resident((1, d)),
            _resident((d, ATTN_WIDTH)),
            _resident((d, ATTN_WIDTH)),
            _resident((d, ATTN_WIDTH)),
            _resident((d, POOL_WIDTH)),
            _resident((d, LANES)),
            _resident((1, N_HEADS)),
        ],
        out_specs=[row(ATTN_WIDTH), row(ATTN_WIDTH), row(ATTN_WIDTH), row(ATTN_WIDTH), vb_spec,
                   row(POOL_WIDTH), row(N_HEADS)],
        out_shape=[
            jax.ShapeDtypeStruct((t, ATTN_WIDTH), BF16),
            jax.ShapeDtypeStruct((t, ATTN_WIDTH), F32),
            jax.ShapeDtypeStruct((t, ATTN_WIDTH), F32),
            jax.ShapeDtypeStruct((t, ATTN_WIDTH), BF16),
            vb_shape,
            jax.ShapeDtypeStruct((t, POOL_WIDTH), F32),
            jax.ShapeDtypeStruct((t, N_HEADS), F32),
        ],
        compiler_params=_params(("parallel",)),
        name="inproj",
    )(x, g, wq, wk, wv, wu, wf, bforget)


def _cumsum_kernel(x_ref, o_ref, buf_ref, *, pad):
    n = x_ref.shape[1]
    buf_ref[pl.ds(0, pad), :] = jnp.zeros((pad, x_ref.shape[2]), F32)
    acc = x_ref[0]
    shift = 1
    while shift < n:
        buf_ref[pl.ds(pad, n), :] = acc
        acc = acc + buf_ref[pl.ds(pad - shift, n), :]
        shift *= 2
    o_ref[0] = acc


def _cumsum_rows(x):
    b, n, h = x.shape
    pad = 1
    while pad < n:
        pad *= 2
    pad //= 2
    pad = max(pad, 8)
    return pl.pallas_call(
        functools.partial(_cumsum_kernel, pad=pad),
        grid=(b,),
        in_specs=[pl.BlockSpec((1, n, h), lambda i: (i, 0, 0))],
        out_specs=pl.BlockSpec((1, n, h), lambda i: (i, 0, 0)),
        out_shape=jax.ShapeDtypeStruct((b, n, h), F32),
        scratch_shapes=[pltpu.VMEM((pad + n, h), F32)],
        compiler_params=_params(("parallel",)),
        name="cumsum",
    )(x)


def _softmax_step(carry, q, k, v, cq, ck, mask, scale):
    m, l, acc = carry
    s = lax.dot_general(q, k, (((1,), (1,)), ((), ())), preferred_element_type=F32) * scale
    logit = s + (cq - ck)
    if mask is not None:
        logit = jnp.where(mask, logit, NEG_INF)
    m_new = jnp.maximum(m, jnp.max(logit, axis=1, keepdims=True))
    alpha = jnp.exp(m - m_new)
    p = jnp.exp(logit - m_new)
    l = alpha * l + jnp.sum(p, axis=1, keepdims=True)
    acc = alpha * acc + jnp.dot(p.astype(BF16), v, preferred_element_type=F32)
    return m_new, l, acc


def _softmax_init(tq):
    return (jnp.full((tq, 1), NEG_INF, F32), jnp.zeros((tq, 1), F32), jnp.zeros((tq, HEAD_DIM), F32))


def _attn_prompt_kernel(q_ref, k_ref, vt_ref, c_ref, ct_ref, o_ref, m_ref, l_ref, acc_ref, cq_ref,
                        cs_ref, a_ref, *, tq, nblk, scale):
    qi = pl.program_id(1)
    scale2 = scale * LOG2E
    key = lax.broadcasted_iota(jnp.int32, (tq, tq), 0)
    qry = lax.broadcasted_iota(jnp.int32, (tq, tq), 1)
    diag_mask = key <= qry
    for h in range(N_HEADS):
        m_ref[h] = jnp.full((1, tq), NEG_INF, F32)
        l_ref[h] = jnp.zeros((1, tq), F32)
        acc_ref[h] = jnp.zeros((HEAD_DIM, tq), F32)
        cq_ref[h] = ct_ref[0, pl.ds(h * nblk + qi, 1), :] * LOG2E

    def block(kb, mask):
        k0 = pl.multiple_of(kb * tq, tq)
        cs_ref[...] = c_ref[0, pl.ds(k0, tq), :] * LOG2E
        col_max = []
        for h in range(N_HEADS):
            hs = slice(h * HEAD_DIM, (h + 1) * HEAD_DIM)
            st = lax.dot_general(k_ref[pl.ds(k0, tq), hs], q_ref[:, hs], (((1,), (1,)), ((), ())),
                                 preferred_element_type=F32)
            a = st * scale2 - cs_ref[:, h:h + 1]
            if mask is not None:
                a = jnp.where(mask, a, NEG_INF)
            a_ref[h] = a
            col_max.append(jnp.max(a, axis=0, keepdims=True))
        for h in range(N_HEADS):
            hs = slice(h * HEAD_DIM, (h + 1) * HEAD_DIM)
            m = m_ref[h]
            m_new = jnp.maximum(m, col_max[h] + cq_ref[h])
            alpha = jnp.exp2(m - m_new)
            p = jnp.exp2(a_ref[h] - (m_new - cq_ref[h]))
            l_ref[h] = alpha * l_ref[h] + jnp.sum(p, axis=0, keepdims=True)
            acc_ref[h] = alpha * acc_ref[h] + jnp.dot(vt_ref[kb, hs, :], p.astype(BF16),
                                                      preferred_element_type=F32)
            m_ref[h] = m_new

    def body(kb, carry):
        block(kb, None)
        return carry

    lax.fori_loop(0, qi, body, 0)
    block(qi, diag_mask)
    for h in range(N_HEADS):
        hs = slice(h * HEAD_DIM, (h + 1) * HEAD_DIM)
        o_ref[:, hs] = (acc_ref[h] / l_ref[h]).T.astype(BF16)


def _attn_prompt(qb, kb, vt, c, *, batch, seq, tq=ATTN_BLOCK):
    nblk = seq // tq
    ct = jnp.swapaxes(c, 1, 2).reshape(batch, N_HEADS * nblk, tq)
    return pl.pallas_call(
        functools.partial(_attn_prompt_kernel, tq=tq, nblk=nblk, scale=HEAD_DIM ** -0.5),
        grid=(batch, nblk),
        in_specs=[
            pl.BlockSpec((tq, ATTN_WIDTH), lambda b, i: (b * nblk + i, 0)),
            pl.BlockSpec((seq, ATTN_WIDTH), lambda b, i: (b, 0)),
            pl.BlockSpec((nblk, ATTN_WIDTH, tq), lambda b, i: (b, 0, 0)),
            pl.BlockSpec((1, seq, N_HEADS), lambda b, i: (b, 0, 0)),
            pl.BlockSpec((1, N_HEADS * nblk, tq), lambda b, i: (b, 0, 0)),
        ],
        out_specs=pl.BlockSpec((tq, ATTN_WIDTH), lambda b, i: (b * nblk + i, 0)),
        out_shape=jax.ShapeDtypeStruct((batch * seq, ATTN_WIDTH), BF16),
        scratch_shapes=[pltpu.VMEM((N_HEADS, 1, tq), F32), pltpu.VMEM((N_HEADS, 1, tq), F32),
                        pltpu.VMEM((N_HEADS, HEAD_DIM, tq), F32), pltpu.VMEM((N_HEADS, 1, tq), F32),
                        pltpu.VMEM((tq, N_HEADS), F32), pltpu.VMEM((N_HEADS, tq, tq), F32)],
        compiler_params=_params(("parallel", "arbitrary")),
        name="attn_prompt",
    )(qb, kb, vt, c, ct)


def _attn_sample_kernel(q_ref, kp_ref, vp_ref, kn_ref, vn_ref, cq_ref, ctp_ref, ctn_ref, o_ref, *, tq, scale):
    row = lax.broadcasted_iota(jnp.int32, (tq, tq), 0)
    col = lax.broadcasted_iota(jnp.int32, (tq, tq), 1)
    diag_mask = col <= row
    for h in range(N_HEADS):
        hs = slice(h * HEAD_DIM, (h + 1) * HEAD_DIM)
        q = q_ref[:, hs]
        cq = cq_ref[0, :, h:h + 1]
        carry = _softmax_init(tq)
        carry = _softmax_step(carry, q, kp_ref[0, :, hs].astype(BF16), vp_ref[0, :, hs].astype(BF16),
                              cq, ctp_ref[0, h:h + 1, :], None, scale)
        _, l, acc = _softmax_step(carry, q, kn_ref[:, hs], vn_ref[:, hs],
                                  cq, ctn_ref[0, h:h + 1, :], diag_mask, scale)
        o_ref[:, hs] = (acc / l).astype(BF16)


def _attn_sample(qb, kb, vb, k_cache, v_cache, c_all, *, layer, batch, tq, past):
    cq = c_all[:, past:, :]
    ct = jnp.swapaxes(c_all, 1, 2)
    ct_past, ct_new = ct[:, :, :past], ct[:, :, past:]
    new = lambda: pl.BlockSpec((tq, ATTN_WIDTH), lambda b: (b, 0))
    old = lambda: pl.BlockSpec((1, past, ATTN_WIDTH), lambda b: (layer * batch + b, 0, 0))
    return pl.pallas_call(
        functools.partial(_attn_sample_kernel, tq=tq, scale=HEAD_DIM ** -0.5),
        grid=(batch,),
        in_specs=[
            new(), old(), old(), new(), new(),
            pl.BlockSpec((1, tq, N_HEADS), lambda b: (b, 0, 0)),
            pl.BlockSpec((1, N_HEADS, past), lambda b: (b, 0, 0)),
            pl.BlockSpec((1, N_HEADS, tq), lambda b: (b, 0, 0)),
        ],
        out_specs=new(),
        out_shape=jax.ShapeDtypeStruct((batch * tq, ATTN_WIDTH), BF16),
        compiler_params=_params(("parallel",)),
        name="attn_sample",
    )(qb, k_cache, v_cache, kb, vb, cq, ct_past, ct_new)


def _pool_kernel(u_ref, prev_ref, d_ref, b0_ref, b1_ref, *, ts, pos0, zero_first):
    s = pl.program_id(1)
    prev = prev_ref[0]
    if zero_first:
        prev = jnp.where(s == 0, 0.0, prev)
    cur = u_ref[0]
    n = HIST_ROWS + ts
    zero_rows = jnp.zeros((POOL_PAD, cur.shape[1]), F32)
    b0_ref[pl.ds(0, POOL_PAD), :] = zero_rows
    b1_ref[pl.ds(0, POOL_PAD), :] = zero_rows
    b0_ref[pl.ds(POOL_PAD, HIST_ROWS), :] = prev
    b0_ref[pl.ds(POOL_PAD + HIST_ROWS, ts), :] = cur
    pos = pos0 + s * ts + lax.broadcasted_iota(jnp.int32, (ts, 1), 0)
    for g, w in enumerate(POOL_WINDOWS):
        gs = slice(g * POOL_GROUP, (g + 1) * POOL_GROUP)
        src, dst = b0_ref, b1_ref
        shift = 1
        while True:
            win = src[pl.ds(POOL_PAD, n), gs] + src[pl.ds(POOL_PAD - shift, n), gs]
            shift *= 2
            if shift == w:
                break
            dst[pl.ds(POOL_PAD, n), gs] = win
            src, dst = dst, src
        cnt = jnp.minimum(w, pos + 1).astype(F32)
        d_ref[0, :, gs] = (win[HIST_ROWS:] / cnt - cur[:, gs]).astype(BF16)


def _pool(u, prev, *, ts, pos0, zero_first, prev_from_u):
    b, n, c = u.shape
    steps = n // ts
    if prev_from_u:
        prev_map = lambda i, s: (i, jnp.maximum(s * (ts // HIST_ROWS) - 1, 0), 0)
    else:
        prev_map = lambda i, s: (i, 0, 0)
    return pl.pallas_call(
        functools.partial(_pool_kernel, ts=ts, pos0=pos0, zero_first=zero_first),
        grid=(b, steps),
        in_specs=[
            pl.BlockSpec((1, ts, c), lambda i, s: (i, s, 0)),
            pl.BlockSpec((1, HIST_ROWS, c), prev_map),
        ],
        out_specs=pl.BlockSpec((1, ts, c), lambda i, s: (i, s, 0)),
        out_shape=jax.ShapeDtypeStruct((b, n, c), BF16),
        scratch_shapes=[pltpu.VMEM((POOL_PAD + HIST_ROWS + ts, c), F32)] * 2,
        compiler_params=_params(("parallel", "arbitrary")),
        name="pool",
    )(u, prev)


def _merge_kernel(x_ref, g_ref, o_ref, d_ref, wga_ref, wgb_ref, wat_ref, wpl_ref, ps_ref, wo_ref,
                  y_ref, h_ref):
    n = pl.program_id(1)

    @pl.when(n == 0)
    def _():
        x = x_ref[...]
        h_ref[...] = _rmsnorm(x, g_ref[...]).astype(BF16)
        y_ref[...] = x

    h = h_ref[...]
    gate_a = jax.nn.sigmoid(jnp.dot(h, wga_ref[...], preferred_element_type=F32))
    gate_b = jax.nn.sigmoid(jnp.dot(h, wgb_ref[...], preferred_element_type=F32))
    attn = jnp.dot(o_ref[...], wat_ref[...], preferred_element_type=F32)
    pool = jnp.dot(d_ref[...], wpl_ref[0], preferred_element_type=F32) * ps_ref[...]
    mixed = (gate_a * attn + gate_b * pool).astype(BF16)
    y_ref[...] += jnp.dot(mixed, wo_ref[...], preferred_element_type=F32)


def _merge(x, g, o, d, wga, wgb, wat, wpl, ps, wo, *, tm=512):
    t, dm = x.shape
    tn = POOL_OUT_GROUP
    return pl.pallas_call(
        _merge_kernel,
        grid=(t // tm, dm // tn),
        in_specs=[
            pl.BlockSpec((tm, dm), lambda i, n: (i, 0)),
            pl.BlockSpec((1, dm), lambda i, n: (0, 0)),
            pl.BlockSpec((tm, ATTN_WIDTH), lambda i, n: (i, 0)),
            pl.BlockSpec((tm, POOL_GROUP), lambda i, n: (i, n)),
            pl.BlockSpec((dm, tn), lambda i, n: (0, n)),
            pl.BlockSpec((dm, tn), lambda i, n: (0, n)),
            pl.BlockSpec((ATTN_WIDTH, tn), lambda i, n: (0, n)),
            pl.BlockSpec((1, POOL_GROUP, tn), lambda i, n: (n, 0, 0)),
            pl.BlockSpec((1, tn), lambda i, n: (0, n)),
            pl.BlockSpec((tn, dm), lambda i, n: (n, 0)),
        ],
        out_specs=pl.BlockSpec((tm, dm), lambda i, n: (i, 0)),
        out_shape=jax.ShapeDtypeStruct((t, dm), F32),
        scratch_shapes=[pltpu.VMEM((tm, dm), BF16)],
        compiler_params=_params(("parallel", "arbitrary")),
        name="merge",
    )(x, g, o, d, wga, wgb, wat, wpl, ps, wo)


def _layer_weights(l, ffn1_norm, ffn1_w_gate, ffn1_w_up, ffn1_w_down, mix_norm, w_in, b_forget, w_branch_attn,
                   w_pool_group, pool_scale, w_out, ffn2_norm, ffn2_w_gate, ffn2_w_up, ffn2_w_down):
    a = ATTN_WIDTH
    wi = w_in[l]
    f0, u0 = 3 * a, 3 * a + N_HEADS
    ga0 = u0 + POOL_WIDTH
    gb0 = ga0 + D_MODEL
    wf = jnp.pad(wi[:, f0:u0], ((0, 0), (0, LANES - N_HEADS)))
    return dict(
        n1=ffn1_norm[l][None], g1=ffn1_w_gate[l].astype(BF16), u1=ffn1_w_up[l].astype(BF16),
        d1=(0.5 * ffn1_w_down[l]).astype(BF16),
        nm=mix_norm[l][None],
        wq=wi[:, :a].astype(BF16), wk=wi[:, a:2 * a].astype(BF16), wv=wi[:, 2 * a:3 * a].astype(BF16),
        wf=wf.astype(BF16), wu=wi[:, u0:ga0].astype(BF16),
        wga=wi[:, ga0:gb0].astype(BF16), wgb=wi[:, gb0:].astype(BF16),
        bf=b_forget[l][None],
        wat=w_branch_attn[l].astype(BF16), wpl=w_pool_group[l].astype(BF16), ps=pool_scale[l][None],
        wo=w_out[l].astype(BF16),
        n2=ffn2_norm[l][None], g2=ffn2_w_gate[l].astype(BF16), u2=ffn2_w_up[l].astype(BF16),
        d2=(0.5 * ffn2_w_down[l]).astype(BF16),
    )


def _layer(x, w, final_norm, *, batch, seq, past, last):
    x1 = _ffn(x, w["n1"], w["g1"], w["u1"], w["d1"], w["n1"], final_norm=False)
    qb, k, v, kb, vb, u, logf = _inproj(x1, w["nm"], w["wq"], w["wk"], w["wv"], w["wu"], w["wf"], w["bf"],
                                        transpose_v=past is None)
    logf3 = logf.reshape(batch, seq, N_HEADS)
    u3 = u.reshape(batch, seq, POOL_WIDTH)
    if past is None:
        c = _cumsum_rows(logf3)
        o = _attn_prompt(qb, kb, vb, c, batch=batch, seq=seq)
        d = _pool(u3, u3, ts=512, pos0=0, zero_first=True, prev_from_u=True)
        new_pool = u3[:, seq - POOL_HIST:, :]
    else:
        layer, k_cache, v_cache, logf_past, hist = past
        plen = k_cache.shape[2]
        c_all = _cumsum_rows(jnp.concatenate([logf_past.astype(F32), logf3], axis=1))
        o = _attn_sample(qb, kb, vb, k_cache.reshape(-1, plen, ATTN_WIDTH), v_cache.reshape(-1, plen, ATTN_WIDTH),
                         c_all, layer=layer, batch=batch, tq=seq, past=plen)
        hist_rows = jnp.pad(hist, ((0, 0), (HIST_ROWS - POOL_HIST, 0), (0, 0)))
        d = _pool(u3, hist_rows, ts=seq, pos0=plen, zero_first=False, prev_from_u=False)
        new_pool = jnp.concatenate([hist, u3], axis=1)[:, -POOL_HIST:, :]
    x2 = _merge(x1, w["nm"], o, d.reshape(batch * seq, POOL_WIDTH), w["wga"], w["wgb"], w["wat"], w["wpl"],
                w["ps"], w["wo"])
    x3 = _ffn(x2, w["n2"], w["g2"], w["u2"], w["d2"], final_norm, final_norm=last)
    return (x3, k.reshape(batch, seq, N_HEADS, HEAD_DIM), v.reshape(batch, seq, N_HEADS, HEAD_DIM), logf3, new_pool)


def kernel(x_prompt, x_sample, cache_k, cache_v, cache_logf, state_pool, ffn1_norm, ffn1_w_gate, ffn1_w_up,
           ffn1_w_down, mix_norm, w_in, b_forget, w_branch_attn, w_pool_group, pool_scale, w_out, ffn2_norm,
           ffn2_w_gate, ffn2_w_up, ffn2_w_down, final_norm):
    depth = w_in.shape[0]
    bp, sp, _ = x_prompt.shape
    bs, ss, _ = x_sample.shape
    hp = x_prompt.reshape(bp * sp, D_MODEL)
    hs = x_sample.reshape(bs * ss, D_MODEL)
    fn = final_norm[None]
    outs = [[] for _ in range(8)]
    for l in range(depth):
        w = _layer_weights(l, ffn1_norm, ffn1_w_gate, ffn1_w_up, ffn1_w_down, mix_norm, w_in, b_forget,
                           w_branch_attn, w_pool_group, pool_scale, w_out, ffn2_norm, ffn2_w_gate, ffn2_w_up,
                           ffn2_w_down)
        last = l == depth - 1
        hp, k1, v1, l1, p1 = _layer(hp, w, fn, batch=bp, seq=sp, past=None, last=last)
        hs, k2, v2, l2, p2 = _layer(hs, w, fn, batch=bs, seq=ss,
                                    past=(l, cache_k, cache_v, cache_logf[l], state_pool[l]), last=last)
        for dst, val in zip(outs, (k1, v1, l1, p1, k2, v2, l2, p2)):
            dst.append(val)
    stacked = [jnp.stack(o) for o in outs]
    return (hp.reshape(bp, sp, D_MODEL), hs.reshape(bs, ss, D_MODEL), *stacked)
```

```python
import functools

import jax
import jax.numpy as jnp
from jax import lax
from jax.experimental import pallas as pl
from jax.experimental.pallas import tpu as pltpu

D_MODEL = 2048
N_HEADS = 8
HEAD_DIM = 128
ATTN_WIDTH = N_HEADS * HEAD_DIM
POOL_WINDOWS = (2, 4, 8, 16)
POOL_WIDTH = D_MODEL // 2
POOL_GROUP = POOL_WIDTH // len(POOL_WINDOWS)
POOL_OUT_GROUP = D_MODEL // len(POOL_WINDOWS)
POOL_HIST = max(POOL_WINDOWS) - 1
RMS_EPS = 1e-6
NEG_INF = -1e30
LOG2E = 1.4426950408889634

LANES = 128
HIST_ROWS = 16
POOL_PAD = 8
ATTN_BLOCK = 256
VMEM_LIMIT = 60 * 1024 * 1024

F32 = jnp.float32
BF16 = jnp.bfloat16


def _params(semantics):
    return pltpu.CompilerParams(dimension_semantics=semantics, vmem_limit_bytes=VMEM_LIMIT)


def _rmsnorm(x, g):
    r = lax.rsqrt(jnp.mean(x * x, axis=-1, keepdims=True) + RMS_EPS)
    return (x * r) * g


def _resident(shape):
    return pl.BlockSpec(shape, lambda *_: (0,) * len(shape), pipeline_mode=pl.Buffered(1))


def _ffn_kernel(x_ref, g_ref, wg_ref, wu_ref, wd_ref, fn_ref, o_ref, h_ref, *, final_norm):
    j = pl.program_id(1)

    @pl.when(j == 0)
    def _():
        x = x_ref[...]
        h_ref[...] = _rmsnorm(x, g_ref[...]).astype(BF16)
        o_ref[...] = x

    h = h_ref[...]
    gate = jnp.dot(h, wg_ref[...], preferred_element_type=F32)
    up = jnp.dot(h, wu_ref[...], preferred_element_type=F32)
    act = (gate * jax.nn.sigmoid(gate) * up).astype(BF16)
    o_ref[...] += jnp.dot(act, wd_ref[...], preferred_element_type=F32)

    if final_norm:
        @pl.when(j == pl.num_programs(1) - 1)
        def _():
            o_ref[...] = _rmsnorm(o_ref[...], fn_ref[...])


def _ffn(x, g, wg, wu, wd, fn, *, final_norm, tm=1024, tf=512):
    t, d = x.shape
    f = wg.shape[1]
    return pl.pallas_call(
        functools.partial(_ffn_kernel, final_norm=final_norm),
        grid=(t // tm, f // tf),
        in_specs=[
            pl.BlockSpec((tm, d), lambda i, j: (i, 0)),
            pl.BlockSpec((1, d), lambda i, j: (0, 0)),
            pl.BlockSpec((d, tf), lambda i, j: (0, j)),
            pl.BlockSpec((d, tf), lambda i, j: (0, j)),
            pl.BlockSpec((tf, d), lambda i, j: (j, 0)),
            pl.BlockSpec((1, d), lambda i, j: (0, 0)),
        ],
        out_specs=pl.BlockSpec((tm, d), lambda i, j: (i, 0)),
        out_shape=jax.ShapeDtypeStruct((t, d), F32),
        scratch_shapes=[pltpu.VMEM((tm, d), BF16)],
        compiler_params=_params(("parallel", "arbitrary")),
        name="ffn",
    )(x, g, wg, wu, wd, fn)


def _running_sum(x, buf_ref, pad):
    n = x.shape[0]
    shift = 1
    while shift < n:
        buf_ref[pl.ds(pad, n), :] = x
        x = x + buf_ref[pl.ds(pad - shift, n), :]
        shift *= 2
    return x


def _largest_shift(n):
    shift = 1
    while shift * 2 < n:
        shift *= 2
    return max(shift, 8)


def _pool_rows(prev, cur, pos, b0_ref, b1_ref):
    ts = cur.shape[0]
    n = HIST_ROWS + ts
    zero_rows = jnp.zeros((POOL_PAD, cur.shape[1]), F32)
    b0_ref[pl.ds(0, POOL_PAD), :] = zero_rows
    b1_ref[pl.ds(0, POOL_PAD), :] = zero_rows
    b0_ref[pl.ds(POOL_PAD, HIST_ROWS), :] = prev
    b0_ref[pl.ds(POOL_PAD + HIST_ROWS, ts), :] = cur
    out = []
    for g, w in enumerate(POOL_WINDOWS):
        gs = slice(g * POOL_GROUP, (g + 1) * POOL_GROUP)
        src, dst = b0_ref, b1_ref
        shift = 1
        while True:
            win = src[pl.ds(POOL_PAD, n), gs] + src[pl.ds(POOL_PAD - shift, n), gs]
            shift *= 2
            if shift == w:
                break
            dst[pl.ds(POOL_PAD, n), gs] = win
            src, dst = dst, src
        cnt = jnp.minimum(w, pos + 1).astype(F32)
        out.append((win[HIST_ROWS:] / cnt - cur[:, gs]).astype(BF16))
    return out


def _log_sigmoid(x):
    return jnp.minimum(x, 0.0) - jnp.log1p(jnp.exp(-jnp.abs(x)))


def _inproj_kernel(x_ref, g_ref, wq_ref, wk_ref, wv_ref, wu_ref, wf_ref, bf_ref,
                   qb_ref, k_ref, v_ref, kb_ref, vb_ref, u_ref, logf_ref, *rest, tiles_per_seq, csum_pad):
    if tiles_per_seq:
        d_ref, c_ref, hist_ref, total_ref, b0_ref, b1_ref, cbuf_ref = rest
        tile = pl.program_id(0) % tiles_per_seq

        @pl.when(tile == 0)
        def _():
            hist_ref[...] = jnp.zeros_like(hist_ref)
            total_ref[...] = jnp.zeros_like(total_ref)

    h = _rmsnorm(x_ref[...], g_ref[...]).astype(BF16)
    u = jnp.dot(h, wu_ref[...], preferred_element_type=F32)
    u_ref[...] = u
    f = jnp.dot(h, wf_ref[...], preferred_element_type=F32)
    logf = _log_sigmoid(f[:, :N_HEADS] + bf_ref[...])
    logf_ref[...] = logf
    if tiles_per_seq:
        tm = u.shape[0]
        pos = tile * tm + lax.broadcasted_iota(jnp.int32, (tm, 1), 0)
        for g, d in enumerate(_pool_rows(hist_ref[...], u, pos, b0_ref, b1_ref)):
            d_ref[:, g * POOL_GROUP:(g + 1) * POOL_GROUP] = d
        hist_ref[...] = u[tm - HIST_ROWS:, :]
        cbuf_ref[pl.ds(0, csum_pad), :] = jnp.zeros((csum_pad, N_HEADS), F32)
        c = _running_sum(logf, cbuf_ref, csum_pad) + total_ref[...]
        c_ref[...] = c
        total_ref[...] = c[tm - 1:, :]

    qb_ref[...] = jnp.dot(h, wq_ref[...], preferred_element_type=F32).astype(BF16)
    k = jnp.dot(h, wk_ref[...], preferred_element_type=F32)
    k_ref[...] = k
    kb_ref[...] = k.astype(BF16)
    v = jnp.dot(h, wv_ref[...], preferred_element_type=F32)
    v_ref[...] = v
    if tiles_per_seq:
        vb_ref[0] = v.T.astype(BF16)
    else:
        vb_ref[...] = v.astype(BF16)


def _inproj(x, g, wq, wk, wv, wu, wf, bforget, *, seq, tm=ATTN_BLOCK):
    t, d = x.shape
    row = lambda width: pl.BlockSpec((tm, width), lambda i: (i, 0))
    out_specs = [row(ATTN_WIDTH), row(ATTN_WIDTH), row(ATTN_WIDTH), row(ATTN_WIDTH), row(ATTN_WIDTH),
                 row(POOL_WIDTH), row(N_HEADS)]
    out_shape = [
        jax.ShapeDtypeStruct((t, ATTN_WIDTH), BF16),
        jax.ShapeDtypeStruct((t, ATTN_WIDTH), F32),
        jax.ShapeDtypeStruct((t, ATTN_WIDTH), F32),
        jax.ShapeDtypeStruct((t, ATTN_WIDTH), BF16),
        jax.ShapeDtypeStruct((t, ATTN_WIDTH), BF16),
        jax.ShapeDtypeStruct((t, POOL_WIDTH), F32),
        jax.ShapeDtypeStruct((t, N_HEADS), F32),
    ]
    scratch = []
    tiles_per_seq, csum_pad = 0, 0
    if seq is not None:
        tiles_per_seq, csum_pad = seq // tm, _largest_shift(tm)
        out_specs[4] = pl.BlockSpec((1, ATTN_WIDTH, tm), lambda i: (i, 0, 0))
        out_shape[4] = jax.ShapeDtypeStruct((t // tm, ATTN_WIDTH, tm), BF16)
        out_specs += [row(POOL_WIDTH), row(N_HEADS)]
        out_shape += [jax.ShapeDtypeStruct((t, POOL_WIDTH), BF16), jax.ShapeDtypeStruct((t, N_HEADS), F32)]
        scratch = [pltpu.VMEM((HIST_ROWS, POOL_WIDTH), F32), pltpu.VMEM((1, N_HEADS), F32),
                   pltpu.VMEM((POOL_PAD + HIST_ROWS + tm, POOL_WIDTH), F32),
                   pltpu.VMEM((POOL_PAD + HIST_ROWS + tm, POOL_WIDTH), F32),
                   pltpu.VMEM((csum_pad + tm, N_HEADS), F32)]
    return pl.pallas_call(
        functools.partial(_inproj_kernel, tiles_per_seq=tiles_per_seq, csum_pad=csum_pad),
        grid=(t // tm,),
        in_specs=[
            row(d),
            _resident((1, d)),
            _resident((d, ATTN_WIDTH)),
            _resident((d, ATTN_WIDTH)),
            _resident((d, ATTN_WIDTH)),
            _resident((d, POOL_WIDTH)),
            _resident((d, LANES)),
            _resident((1, N_HEADS)),
        ],
        out_specs=out_specs,
        out_shape=out_shape,
        scratch_shapes=scratch,
        compiler_params=_params(("arbitrary",)),
        name="inproj",
    )(x, g, wq, wk, wv, wu, wf, bforget)


def _cumsum_kernel(x_ref, o_ref, buf_ref, *, pad):
    buf_ref[pl.ds(0, pad), :] = jnp.zeros((pad, x_ref.shape[2]), F32)
    o_ref[0] = _running_sum(x_ref[0], buf_ref, pad)


def _cumsum_rows(x):
    b, n, h = x.shape
    pad = _largest_shift(n)
    return pl.pallas_call(
        functools.partial(_cumsum_kernel, pad=pad),
        grid=(b,),
        in_specs=[pl.BlockSpec((1, n, h), lambda i: (i, 0, 0))],
        out_specs=pl.BlockSpec((1, n, h), lambda i: (i, 0, 0)),
        out_shape=jax.ShapeDtypeStruct((b, n, h), F32),
        scratch_shapes=[pltpu.VMEM((pad + n, h), F32)],
        compiler_params=_params(("parallel",)),
        name="cumsum",
    )(x)


def _softmax_step(carry, q, k, v, cq, ck, mask, scale):
    m, l, acc = carry
    s = lax.dot_general(q, k, (((1,), (1,)), ((), ())), preferred_element_type=F32) * scale
    logit = s + (cq - ck)
    if mask is not None:
        logit = jnp.where(mask, logit, NEG_INF)
    m_new = jnp.maximum(m, jnp.max(logit, axis=1, keepdims=True))
    alpha = jnp.exp(m - m_new)
    p = jnp.exp(logit - m_new)
    l = alpha * l + jnp.sum(p, axis=1, keepdims=True)
    acc = alpha * acc + jnp.dot(p.astype(BF16), v, preferred_element_type=F32)
    return m_new, l, acc


def _softmax_init(tq):
    return (jnp.full((tq, 1), NEG_INF, F32), jnp.zeros((tq, 1), F32), jnp.zeros((tq, HEAD_DIM), F32))


def _attn_prompt_kernel(q_ref, k_ref, vt_ref, c_ref, ct_ref, o_ref, m_ref, l_ref, acc_ref, cq_ref,
                        cs_ref, a_ref, *, tq, nblk, scale):
    qi = pl.program_id(1)
    scale2 = scale * LOG2E
    key = lax.broadcasted_iota(jnp.int32, (tq, tq), 0)
    qry = lax.broadcasted_iota(jnp.int32, (tq, tq), 1)
    diag_mask = key <= qry
    for h in range(N_HEADS):
        m_ref[h] = jnp.full((1, tq), NEG_INF, F32)
        l_ref[h] = jnp.zeros((1, tq), F32)
        acc_ref[h] = jnp.zeros((HEAD_DIM, tq), F32)
        cq_ref[h] = ct_ref[0, pl.ds(h * nblk + qi, 1), :] * LOG2E

    def block(kb, mask):
        k0 = pl.multiple_of(kb * tq, tq)
        cs_ref[...] = c_ref[0, pl.ds(k0, tq), :] * LOG2E
        col_max = []
        for h in range(N_HEADS):
            hs = slice(h * HEAD_DIM, (h + 1) * HEAD_DIM)
            st = lax.dot_general(k_ref[pl.ds(k0, tq), hs], q_ref[:, hs], (((1,), (1,)), ((), ())),
                                 preferred_element_type=F32)
            a = st * scale2 - cs_ref[:, h:h + 1]
            if mask is not None:
                a = jnp.where(mask, a, NEG_INF)
            a_ref[h] = a
            col_max.append(jnp.max(a, axis=0, keepdims=True))
        for h in range(N_HEADS):
            hs = slice(h * HEAD_DIM, (h + 1) * HEAD_DIM)
            m = m_ref[h]
            m_new = jnp.maximum(m, col_max[h] + cq_ref[h])
            alpha = jnp.exp2(m - m_new)
            p = jnp.exp2(a_ref[h] - (m_new - cq_ref[h]))
            l_ref[h] = alpha * l_ref[h] + jnp.sum(p, axis=0, keepdims=True)
            acc_ref[h] = alpha * acc_ref[h] + jnp.dot(vt_ref[kb, hs, :], p.astype(BF16),
                                                      preferred_element_type=F32)
            m_ref[h] = m_new

    def body(kb, carry):
        block(kb, None)
        return carry

    lax.fori_loop(0, qi, body, 0)
    block(qi, diag_mask)
    for h in range(N_HEADS):
        hs = slice(h * HEAD_DIM, (h + 1) * HEAD_DIM)
        o_ref[:, hs] = (acc_ref[h] / l_ref[h]).T.astype(BF16)


def _attn_prompt(qb, kb, vt, c, *, batch, seq, tq=ATTN_BLOCK):
    nblk = seq // tq
    ct = jnp.swapaxes(c, 1, 2).reshape(batch, N_HEADS * nblk, tq)
    return pl.pallas_call(
        functools.partial(_attn_prompt_kernel, tq=tq, nblk=nblk, scale=HEAD_DIM ** -0.5),
        grid=(batch, nblk),
        in_specs=[
            pl.BlockSpec((tq, ATTN_WIDTH), lambda b, i: (b * nblk + i, 0)),
            pl.BlockSpec((seq, ATTN_WIDTH), lambda b, i: (b, 0)),
            pl.BlockSpec((nblk, ATTN_WIDTH, tq), lambda b, i: (b, 0, 0)),
            pl.BlockSpec((1, seq, N_HEADS), lambda b, i: (b, 0, 0)),
            pl.BlockSpec((1, N_HEADS * nblk, tq), lambda b, i: (b, 0, 0)),
        ],
        out_specs=pl.BlockSpec((tq, ATTN_WIDTH), lambda b, i: (b * nblk + i, 0)),
        out_shape=jax.ShapeDtypeStruct((batch * seq, ATTN_WIDTH), BF16),
        scratch_shapes=[pltpu.VMEM((N_HEADS, 1, tq), F32), pltpu.VMEM((N_HEADS, 1, tq), F32),
                        pltpu.VMEM((N_HEADS, HEAD_DIM, tq), F32), pltpu.VMEM((N_HEADS, 1, tq), F32),
                        pltpu.VMEM((tq, N_HEADS), F32), pltpu.VMEM((N_HEADS, tq, tq), F32)],
        compiler_params=_params(("parallel", "arbitrary")),
        name="attn_prompt",
    )(qb, kb, vt, c, ct)


def _attn_sample_kernel(q_ref, kp_ref, vp_ref, kn_ref, vn_ref, cq_ref, ctp_ref, ctn_ref, o_ref, *, tq, past, scale):
    row = lax.broadcasted_iota(jnp.int32, (tq, tq), 0)
    col = lax.broadcasted_iota(jnp.int32, (tq, tq), 1)
    diag_mask = col <= row
    for h in range(N_HEADS):
        hs = slice(h * HEAD_DIM, (h + 1) * HEAD_DIM)
        q = q_ref[:, hs]
        cq = cq_ref[0, :, h:h + 1]
        carry = _softmax_init(tq)
        head_rows = pl.ds(h, past, stride=N_HEADS)
        carry = _softmax_step(carry, q, kp_ref[0, head_rows, :].astype(BF16), vp_ref[0, head_rows, :].astype(BF16),
                              cq, ctp_ref[0, h:h + 1, :], None, scale)
        _, l, acc = _softmax_step(carry, q, kn_ref[:, hs], vn_ref[:, hs],
                                  cq, ctn_ref[0, h:h + 1, :], diag_mask, scale)
        o_ref[:, hs] = (acc / l).astype(BF16)


def _attn_sample(qb, kb, vb, k_cache, v_cache, c_all, *, layer, batch, tq, past):
    cq = c_all[:, past:, :]
    ct = jnp.swapaxes(c_all, 1, 2)
    ct_past, ct_new = ct[:, :, :past], ct[:, :, past:]
    k_cache = k_cache.reshape(-1, past * N_HEADS, HEAD_DIM)
    v_cache = v_cache.reshape(-1, past * N_HEADS, HEAD_DIM)
    new = lambda: pl.BlockSpec((tq, ATTN_WIDTH), lambda b: (b, 0))
    old = lambda: pl.BlockSpec((1, past * N_HEADS, HEAD_DIM), lambda b: (layer * batch + b, 0, 0))
    return pl.pallas_call(
        functools.partial(_attn_sample_kernel, tq=tq, past=past, scale=HEAD_DIM ** -0.5),
        grid=(batch,),
        in_specs=[
            new(), old(), old(), new(), new(),
            pl.BlockSpec((1, tq, N_HEADS), lambda b: (b, 0, 0)),
            pl.BlockSpec((1, N_HEADS, past), lambda b: (b, 0, 0)),
            pl.BlockSpec((1, N_HEADS, tq), lambda b: (b, 0, 0)),
        ],
        out_specs=new(),
        out_shape=jax.ShapeDtypeStruct((batch * tq, ATTN_WIDTH), BF16),
        compiler_params=_params(("parallel",)),
        name="attn_sample",
    )(qb, k_cache, v_cache, kb, vb, cq, ct_past, ct_new)


def _pool_kernel(u_ref, prev_ref, d_ref, b0_ref, b1_ref, *, pos0):
    ts = u_ref.shape[1]
    pos = pos0 + lax.broadcasted_iota(jnp.int32, (ts, 1), 0)
    for g, d in enumerate(_pool_rows(prev_ref[0], u_ref[0], pos, b0_ref, b1_ref)):
        d_ref[0, :, g * POOL_GROUP:(g + 1) * POOL_GROUP] = d


def _pool(u, prev, *, pos0):
    b, n, c = u.shape
    return pl.pallas_call(
        functools.partial(_pool_kernel, pos0=pos0),
        grid=(b,),
        in_specs=[
            pl.BlockSpec((1, n, c), lambda i: (i, 0, 0)),
            pl.BlockSpec((1, HIST_ROWS, c), lambda i: (i, 0, 0)),
        ],
        out_specs=pl.BlockSpec((1, n, c), lambda i: (i, 0, 0)),
        out_shape=jax.ShapeDtypeStruct((b, n, c), BF16),
        scratch_shapes=[pltpu.VMEM((POOL_PAD + HIST_ROWS + n, c), F32)] * 2,
        compiler_params=_params(("parallel",)),
        name="pool",
    )(u, prev)


def _merge_kernel(x_ref, g_ref, o_ref, d_ref, wga_ref, wgb_ref, wat_ref, wpl_ref, ps_ref, wo_ref,
                  y_ref, h_ref):
    n = pl.program_id(1)

    @pl.when(n == 0)
    def _():
        x = x_ref[...]
        h_ref[...] = _rmsnorm(x, g_ref[...]).astype(BF16)
        y_ref[...] = x

    h = h_ref[...]
    gate_a = jax.nn.sigmoid(jnp.dot(h, wga_ref[...], preferred_element_type=F32))
    gate_b = jax.nn.sigmoid(jnp.dot(h, wgb_ref[...], preferred_element_type=F32))
    attn = jnp.dot(o_ref[...], wat_ref[...], preferred_element_type=F32)
    pool = jnp.dot(d_ref[...], wpl_ref[0], preferred_element_type=F32) * ps_ref[...]
    mixed = (gate_a * attn + gate_b * pool).astype(BF16)
    y_ref[...] += jnp.dot(mixed, wo_ref[...], preferred_element_type=F32)


def _merge(x, g, o, d, wga, wgb, wat, wpl, ps, wo, *, tm=512):
    t, dm = x.shape
    tn = POOL_OUT_GROUP
    return pl.pallas_call(
        _merge_kernel,
        grid=(t // tm, dm // tn),
        in_specs=[
            pl.BlockSpec((tm, dm), lambda i, n: (i, 0)),
            pl.BlockSpec((1, dm), lambda i, n: (0, 0)),
            pl.BlockSpec((tm, ATTN_WIDTH), lambda i, n: (i, 0)),
            pl.BlockSpec((tm, POOL_GROUP), lambda i, n: (i, n)),
            pl.BlockSpec((dm, tn), lambda i, n: (0, n)),
            pl.BlockSpec((dm, tn), lambda i, n: (0, n)),
            pl.BlockSpec((ATTN_WIDTH, tn), lambda i, n: (0, n)),
            pl.BlockSpec((1, POOL_GROUP, tn), lambda i, n: (n, 0, 0)),
            pl.BlockSpec((1, tn), lambda i, n: (0, n)),
            pl.BlockSpec((tn, dm), lambda i, n: (n, 0)),
        ],
        out_specs=pl.BlockSpec((tm, dm), lambda i, n: (i, 0)),
        out_shape=jax.ShapeDtypeStruct((t, dm), F32),
        scratch_shapes=[pltpu.VMEM((tm, dm), BF16)],
        compiler_params=_params(("parallel", "arbitrary")),
        name="merge",
    )(x, g, o, d, wga, wgb, wat, wpl, ps, wo)


def _layer_weights(l, ffn1_norm, ffn1_w_gate, ffn1_w_up, ffn1_w_down, mix_norm, w_in, b_forget, w_branch_attn,
                   w_pool_group, pool_scale, w_out, ffn2_norm, ffn2_w_gate, ffn2_w_up, ffn2_w_down):
    a = ATTN_WIDTH
    wi = w_in[l]
    f0, u0 = 3 * a, 3 * a + N_HEADS
    ga0 = u0 + POOL_WIDTH
    gb0 = ga0 + D_MODEL
    wf = jnp.pad(wi[:, f0:u0], ((0, 0), (0, LANES - N_HEADS)))
    return dict(
        n1=ffn1_norm[l][None], g1=ffn1_w_gate[l].astype(BF16), u1=ffn1_w_up[l].astype(BF16),
        d1=(0.5 * ffn1_w_down[l]).astype(BF16),
        nm=mix_norm[l][None],
        wq=wi[:, :a].astype(BF16), wk=wi[:, a:2 * a].astype(BF16), wv=wi[:, 2 * a:3 * a].astype(BF16),
        wf=wf.astype(BF16), wu=wi[:, u0:ga0].astype(BF16),
        wga=wi[:, ga0:gb0].astype(BF16), wgb=wi[:, gb0:].astype(BF16),
        bf=b_forget[l][None],
        wat=w_branch_attn[l].astype(BF16), wpl=w_pool_group[l].astype(BF16), ps=pool_scale[l][None],
        wo=w_out[l].astype(BF16),
        n2=ffn2_norm[l][None], g2=ffn2_w_gate[l].astype(BF16), u2=ffn2_w_up[l].astype(BF16),
        d2=(0.5 * ffn2_w_down[l]).astype(BF16),
    )


def _layer(x, w, final_norm, *, batch, seq, past, last):
    x1 = _ffn(x, w["n1"], w["g1"], w["u1"], w["d1"], w["n1"], final_norm=False)
    proj = _inproj(x1, w["nm"], w["wq"], w["wk"], w["wv"], w["wu"], w["wf"], w["bf"],
                   seq=seq if past is None else None)
    qb, k, v, kb, vb, u, logf = proj[:7]
    logf3 = logf.reshape(batch, seq, N_HEADS)
    u3 = u.reshape(batch, seq, POOL_WIDTH)
    if past is None:
        d, c = proj[7:]
        o = _attn_prompt(qb, kb, vb, c.reshape(batch, seq, N_HEADS), batch=batch, seq=seq)
        new_pool = u3[:, seq - POOL_HIST:, :]
    else:
        layer, k_cache, v_cache, logf_past, hist = past
        plen = k_cache.shape[2]
        c_all = _cumsum_rows(jnp.concatenate([logf_past.astype(F32), logf3], axis=1))
        o = _attn_sample(qb, kb, vb, k_cache, v_cache, c_all, layer=layer, batch=batch, tq=seq, past=plen)
        hist_rows = jnp.pad(hist, ((0, 0), (HIST_ROWS - POOL_HIST, 0), (0, 0)))
        d = _pool(u3, hist_rows, pos0=plen).reshape(batch * seq, POOL_WIDTH)
        new_pool = jnp.concatenate([hist, u3], axis=1)[:, -POOL_HIST:, :]
    x2 = _merge(x1, w["nm"], o, d, w["wga"], w["wgb"], w["wat"], w["wpl"], w["ps"], w["wo"])
    x3 = _ffn(x2, w["n2"], w["g2"], w["u2"], w["d2"], final_norm, final_norm=last)
    return (x3, k.reshape(batch, seq, N_HEADS, HEAD_DIM), v.reshape(batch, seq, N_HEADS, HEAD_DIM), logf3, new_pool)


def kernel(x_prompt, x_sample, cache_k, cache_v, cache_logf, state_pool, ffn1_norm, ffn1_w_gate, ffn1_w_up,
           ffn1_w_down, mix_norm, w_in, b_forget, w_branch_attn, w_pool_group, pool_scale, w_out, ffn2_norm,
           ffn2_w_gate, ffn2_w_up, ffn2_w_down, final_norm):
    depth = w_in.shape[0]
    bp, sp, _ = x_prompt.shape
    bs, ss, _ = x_sample.shape
    hp = x_prompt.reshape(bp * sp, D_MODEL)
    hs = x_sample.reshape(bs * ss, D_MODEL)
    fn = final_norm[None]
    outs = [[] for _ in range(8)]
    for l in range(depth):
        w = _layer_weights(l, ffn1_norm, ffn1_w_gate, ffn1_w_up, ffn1_w_down, mix_norm, w_in, b_forget,
                           w_branch_attn, w_pool_group, pool_scale, w_out, ffn2_norm, ffn2_w_gate, ffn2_w_up,
                           ffn2_w_down)
        last = l == depth - 1
        hp, k1, v1, l1, p1 = _layer(hp, w, fn, batch=bp, seq=sp, past=None, last=last)
        hs, k2, v2, l2, p2 = _layer(hs, w, fn, batch=bs, seq=ss,
                                    past=(l, cache_k, cache_v, cache_logf[l], state_pool[l]), last=last)
        for dst, val in zip(outs, (k1, v1, l1, p1, k2, v2, l2, p2)):
            dst.append(val)
    stacked = [jnp.stack(o) for o in outs]
    return (hp.reshape(bp, sp, D_MODEL), hs.reshape(bs, ss, D_MODEL), *stacked)
```

```python
import functools

import jax
import jax.numpy as jnp
from jax import lax
from jax.experimental import pallas as pl
from jax.experimental.pallas import tpu as pltpu

D_MODEL = 2048
N_HEADS = 8
HEAD_DIM = 128
ATTN_WIDTH = N_HEADS * HEAD_DIM
POOL_WINDOWS = (2, 4, 8, 16)
POOL_WIDTH = D_MODEL // 2
POOL_GROUP = POOL_WIDTH // len(POOL_WINDOWS)
POOL_OUT_GROUP = D_MODEL // len(POOL_WINDOWS)
POOL_HIST = max(POOL_WINDOWS) - 1
RMS_EPS = 1e-6
NEG_INF = -1e30
LOG2E = 1.4426950408889634

LANES = 128
HIST_ROWS = 16
POOL_PAD = 8
ATTN_BLOCK = 256
FFN_ROW_CHUNK = 256
VMEM_LIMIT = 60 * 1024 * 1024

F32 = jnp.float32
BF16 = jnp.bfloat16


def _params(semantics):
    return pltpu.CompilerParams(dimension_semantics=semantics, vmem_limit_bytes=VMEM_LIMIT)


def _rmsnorm(x, g):
    r = lax.rsqrt(jnp.mean(x * x, axis=-1, keepdims=True) + RMS_EPS)
    return (x * r) * g


def _resident(shape):
    return pl.BlockSpec(shape, lambda *_: (0,) * len(shape), pipeline_mode=pl.Buffered(1))


def _ffn_kernel(x_ref, g_ref, wg_ref, wu_ref, wd_ref, fn_ref, o_ref, h_ref, *, final_norm):
    j = pl.program_id(1)

    def swiglu(h):
        gate = jnp.dot(h, wg_ref[...], preferred_element_type=F32)
        up = jnp.dot(h, wu_ref[...], preferred_element_type=F32)
        act = (gate * jax.nn.sigmoid(gate) * up).astype(BF16)
        return jnp.dot(act, wd_ref[...], preferred_element_type=F32)

    @pl.when(j == 0)
    def _():
        for r in range(0, o_ref.shape[0], FFN_ROW_CHUNK):
            rows = pl.ds(r, FFN_ROW_CHUNK)
            x = x_ref[rows, :]
            h = _rmsnorm(x, g_ref[...]).astype(BF16)
            h_ref[rows, :] = h
            o_ref[rows, :] = x + swiglu(h)

    @pl.when(j > 0)
    def _():
        o_ref[...] += swiglu(h_ref[...])

    if final_norm:
        @pl.when(j == pl.num_programs(1) - 1)
        def _():
            o_ref[...] = _rmsnorm(o_ref[...], fn_ref[...])


def _ffn(x, g, wg, wu, wd, fn, *, final_norm, tm=1024, tf=512):
    t, d = x.shape
    f = wg.shape[1]
    return pl.pallas_call(
        functools.partial(_ffn_kernel, final_norm=final_norm),
        grid=(t // tm, f // tf),
        in_specs=[
            pl.BlockSpec((tm, d), lambda i, j: (i, 0)),
            pl.BlockSpec((1, d), lambda i, j: (0, 0)),
            pl.BlockSpec((d, tf), lambda i, j: (0, j)),
            pl.BlockSpec((d, tf), lambda i, j: (0, j)),
            pl.BlockSpec((tf, d), lambda i, j: (j, 0)),
            pl.BlockSpec((1, d), lambda i, j: (0, 0)),
        ],
        out_specs=pl.BlockSpec((tm, d), lambda i, j: (i, 0)),
        out_shape=jax.ShapeDtypeStruct((t, d), F32),
        scratch_shapes=[pltpu.VMEM((tm, d), BF16)],
        compiler_params=_params(("parallel", "arbitrary")),
        name="ffn",
    )(x, g, wg, wu, wd, fn)


def _running_sum(x, buf_ref, pad):
    n = x.shape[0]
    shift = 1
    while shift < n:
        buf_ref[pl.ds(pad, n), :] = x
        x = x + buf_ref[pl.ds(pad - shift, n), :]
        shift *= 2
    return x


def _largest_shift(n):
    shift = 1
    while shift * 2 < n:
        shift *= 2
    return max(shift, 8)


def _pool_rows(prev, cur, pos, b0_ref, b1_ref):
    ts = cur.shape[0]
    n = HIST_ROWS + ts
    zero_rows = jnp.zeros((POOL_PAD, cur.shape[1]), F32)
    b0_ref[pl.ds(0, POOL_PAD), :] = zero_rows
    b1_ref[pl.ds(0, POOL_PAD), :] = zero_rows
    b0_ref[pl.ds(POOL_PAD, HIST_ROWS), :] = prev
    b0_ref[pl.ds(POOL_PAD + HIST_ROWS, ts), :] = cur
    out = []
    for g, w in enumerate(POOL_WINDOWS):
        gs = slice(g * POOL_GROUP, (g + 1) * POOL_GROUP)
        src, dst = b0_ref, b1_ref
        shift = 1
        while True:
            win = src[pl.ds(POOL_PAD, n), gs] + src[pl.ds(POOL_PAD - shift, n), gs]
            shift *= 2
            if shift == w:
                break
            dst[pl.ds(POOL_PAD, n), gs] = win
            src, dst = dst, src
        cnt = jnp.minimum(w, pos + 1).astype(F32)
        out.append((win[HIST_ROWS:] / cnt - cur[:, gs]).astype(BF16))
    return out


def _log_sigmoid(x):
    return jnp.minimum(x, 0.0) - jnp.log1p(jnp.exp(-jnp.abs(x)))


def _inproj_kernel(x_ref, g_ref, wq_ref, wk_ref, wv_ref, wu_ref, wf_ref, bf_ref,
                   qb_ref, k_ref, v_ref, kb_ref, vb_ref, u_ref, logf_ref, hn_ref, *rest, tiles_per_seq, csum_pad):
    if tiles_per_seq:
        d_ref, c_ref, hist_ref, total_ref, b0_ref, b1_ref, cbuf_ref = rest
        tile = pl.program_id(0) % tiles_per_seq

        @pl.when(tile == 0)
        def _():
            hist_ref[...] = jnp.zeros_like(hist_ref)
            total_ref[...] = jnp.zeros_like(total_ref)

    h = _rmsnorm(x_ref[...], g_ref[...]).astype(BF16)
    hn_ref[...] = h
    u = jnp.dot(h, wu_ref[...], preferred_element_type=F32)
    u_ref[...] = u
    f = jnp.dot(h, wf_ref[...], preferred_element_type=F32)
    logf = _log_sigmoid(f[:, :N_HEADS] + bf_ref[...])
    logf_ref[...] = logf
    if tiles_per_seq:
        tm = u.shape[0]
        pos = tile * tm + lax.broadcasted_iota(jnp.int32, (tm, 1), 0)
        for g, d in enumerate(_pool_rows(hist_ref[...], u, pos, b0_ref, b1_ref)):
            d_ref[:, g * POOL_GROUP:(g + 1) * POOL_GROUP] = d
        hist_ref[...] = u[tm - HIST_ROWS:, :]
        cbuf_ref[pl.ds(0, csum_pad), :] = jnp.zeros((csum_pad, N_HEADS), F32)
        c = _running_sum(logf, cbuf_ref, csum_pad) + total_ref[...]
        c_ref[...] = c
        total_ref[...] = c[tm - 1:, :]

    qb_ref[...] = jnp.dot(h, wq_ref[...], preferred_element_type=F32).astype(BF16)
    k = jnp.dot(h, wk_ref[...], preferred_element_type=F32)
    k_ref[...] = k
    kb_ref[...] = k.astype(BF16)
    v = jnp.dot(h, wv_ref[...], preferred_element_type=F32)
    v_ref[...] = v
    if tiles_per_seq:
        vb_ref[0] = v.T.astype(BF16)
    else:
        vb_ref[...] = v.astype(BF16)


def _inproj(x, g, wq, wk, wv, wu, wf, bforget, *, seq, tm=ATTN_BLOCK):
    t, d = x.shape
    row = lambda width: pl.BlockSpec((tm, width), lambda i: (i, 0))
    out_specs = [row(ATTN_WIDTH), row(ATTN_WIDTH), row(ATTN_WIDTH), row(ATTN_WIDTH), row(ATTN_WIDTH),
                 row(POOL_WIDTH), row(N_HEADS), row(d)]
    out_shape = [
        jax.ShapeDtypeStruct((t, ATTN_WIDTH), BF16),
        jax.ShapeDtypeStruct((t, ATTN_WIDTH), F32),
        jax.ShapeDtypeStruct((t, ATTN_WIDTH), F32),
        jax.ShapeDtypeStruct((t, ATTN_WIDTH), BF16),
        jax.ShapeDtypeStruct((t, ATTN_WIDTH), BF16),
        jax.ShapeDtypeStruct((t, POOL_WIDTH), F32),
        jax.ShapeDtypeStruct((t, N_HEADS), F32),
        jax.ShapeDtypeStruct((t, d), BF16),
    ]
    scratch = []
    tiles_per_seq, csum_pad = 0, 0
    if seq is not None:
        tiles_per_seq, csum_pad = seq // tm, _largest_shift(tm)
        out_specs[4] = pl.BlockSpec((1, ATTN_WIDTH, tm), lambda i: (i, 0, 0))
        out_shape[4] = jax.ShapeDtypeStruct((t // tm, ATTN_WIDTH, tm), BF16)
        out_specs += [row(POOL_WIDTH), row(N_HEADS)]
        out_shape += [jax.ShapeDtypeStruct((t, POOL_WIDTH), BF16), jax.ShapeDtypeStruct((t, N_HEADS), F32)]
        scratch = [pltpu.VMEM((HIST_ROWS, POOL_WIDTH), F32), pltpu.VMEM((1, N_HEADS), F32),
                   pltpu.VMEM((POOL_PAD + HIST_ROWS + tm, POOL_WIDTH), F32),
                   pltpu.VMEM((POOL_PAD + HIST_ROWS + tm, POOL_WIDTH), F32),
                   pltpu.VMEM((csum_pad + tm, N_HEADS), F32)]
    return pl.pallas_call(
        functools.partial(_inproj_kernel, tiles_per_seq=tiles_per_seq, csum_pad=csum_pad),
        grid=(t // tm,),
        in_specs=[
            row(d),
            _resident((1, d)),
            _resident((d, ATTN_WIDTH)),
            _resident((d, ATTN_WIDTH)),
            _resident((d, ATTN_WIDTH)),
            _resident((d, POOL_WIDTH)),
            _resident((d, LANES)),
            _resident((1, N_HEADS)),
        ],
        out_specs=out_specs,
        out_shape=out_shape,
        scratch_shapes=scratch,
        compiler_params=_params(("arbitrary",)),
        name="inproj",
    )(x, g, wq, wk, wv, wu, wf, bforget)


def _cumsum_kernel(past_ref, new_ref, o_ref, buf_ref, *, pad):
    buf_ref[pl.ds(0, pad), :] = jnp.zeros((pad, new_ref.shape[2]), F32)
    rows = jnp.concatenate([past_ref[0].astype(F32), new_ref[0]], axis=0)
    o_ref[0] = _running_sum(rows, buf_ref, pad)


def _cumsum_rows(past, new, *, layer):
    b, t, h = new.shape
    p = past.shape[1]
    n = p + t
    pad = _largest_shift(n)
    return pl.pallas_call(
        functools.partial(_cumsum_kernel, pad=pad),
        grid=(b,),
        in_specs=[pl.BlockSpec((1, p, h), lambda i: (layer * b + i, 0, 0)),
                  pl.BlockSpec((1, t, h), lambda i: (i, 0, 0))],
        out_specs=pl.BlockSpec((1, n, h), lambda i: (i, 0, 0)),
        out_shape=jax.ShapeDtypeStruct((b, n, h), F32),
        scratch_shapes=[pltpu.VMEM((pad + n, h), F32)],
        compiler_params=_params(("parallel",)),
        name="cumsum",
    )(past, new)


def _softmax_step(carry, q, k, v, cq, ck, mask, scale):
    m, l, acc = carry
    s = lax.dot_general(q, k, (((1,), (1,)), ((), ())), preferred_element_type=F32) * scale
    logit = s + (cq - ck)
    if mask is not None:
        logit = jnp.where(mask, logit, NEG_INF)
    m_new = jnp.maximum(m, jnp.max(logit, axis=1, keepdims=True))
    alpha = jnp.exp(m - m_new)
    p = jnp.exp(logit - m_new)
    l = alpha * l + jnp.sum(p, axis=1, keepdims=True)
    acc = alpha * acc + jnp.dot(p.astype(BF16), v, preferred_element_type=F32)
    return m_new, l, acc


def _softmax_init(tq):
    return (jnp.full((tq, 1), NEG_INF, F32), jnp.zeros((tq, 1), F32), jnp.zeros((tq, HEAD_DIM), F32))


def _attn_prompt_kernel(q_ref, k_ref, vt_ref, c_ref, ct_ref, o_ref, m_ref, l_ref, acc_ref, cq_ref,
                        cs_ref, a_ref, *, tq, nblk, scale):
    qi = pl.program_id(1)
    scale2 = scale * LOG2E
    key = lax.broadcasted_iota(jnp.int32, (tq, tq), 0)
    qry = lax.broadcasted_iota(jnp.int32, (tq, tq), 1)
    diag_mask = key <= qry
    for h in range(N_HEADS):
        m_ref[h] = jnp.full((1, tq), NEG_INF, F32)
        l_ref[h] = jnp.zeros((1, tq), F32)
        acc_ref[h] = jnp.zeros((HEAD_DIM, tq), F32)
        cq_ref[h] = ct_ref[0, pl.ds(h * nblk + qi, 1), :] * LOG2E

    def block(kb, mask):
        k0 = pl.multiple_of(kb * tq, tq)
        cs_ref[...] = c_ref[0, pl.ds(k0, tq), :] * LOG2E
        col_max = []
        for h in range(N_HEADS):
            hs = slice(h * HEAD_DIM, (h + 1) * HEAD_DIM)
            st = lax.dot_general(k_ref[pl.ds(k0, tq), hs], q_ref[:, hs], (((1,), (1,)), ((), ())),
                                 preferred_element_type=F32)
            a = st * scale2 - cs_ref[:, h:h + 1]
            if mask is not None:
                a = jnp.where(mask, a, NEG_INF)
            a_ref[h] = a
            col_max.append(jnp.max(a, axis=0, keepdims=True))
        for h in range(N_HEADS):
            hs = slice(h * HEAD_DIM, (h + 1) * HEAD_DIM)
            m = m_ref[h]
            m_new = jnp.maximum(m, col_max[h] + cq_ref[h])
            alpha = jnp.exp2(m - m_new)
            p = jnp.exp2(a_ref[h] - (m_new - cq_ref[h]))
            l_ref[h] = alpha * l_ref[h] + jnp.sum(p, axis=0, keepdims=True)
            acc_ref[h] = alpha * acc_ref[h] + jnp.dot(vt_ref[kb, hs, :], p.astype(BF16),
                                                      preferred_element_type=F32)
            m_ref[h] = m_new

    def body(kb, carry):
        block(kb, None)
        return carry

    lax.fori_loop(0, qi, body, 0)
    block(qi, diag_mask)
    for h in range(N_HEADS):
        hs = slice(h * HEAD_DIM, (h + 1) * HEAD_DIM)
        o_ref[:, hs] = (acc_ref[h] / l_ref[h]).T.astype(BF16)


def _attn_prompt(qb, kb, vt, c, *, batch, seq, tq=ATTN_BLOCK):
    nblk = seq // tq
    ct = jnp.swapaxes(c, 1, 2).reshape(batch, N_HEADS * nblk, tq)
    return pl.pallas_call(
        functools.partial(_attn_prompt_kernel, tq=tq, nblk=nblk, scale=HEAD_DIM ** -0.5),
        grid=(batch, nblk),
        in_specs=[
            pl.BlockSpec((tq, ATTN_WIDTH), lambda b, i: (b * nblk + i, 0)),
            pl.BlockSpec((seq, ATTN_WIDTH), lambda b, i: (b, 0)),
            pl.BlockSpec((nblk, ATTN_WIDTH, tq), lambda b, i: (b, 0, 0)),
            pl.BlockSpec((1, seq, N_HEADS), lambda b, i: (b, 0, 0)),
            pl.BlockSpec((1, N_HEADS * nblk, tq), lambda b, i: (b, 0, 0)),
        ],
        out_specs=pl.BlockSpec((tq, ATTN_WIDTH), lambda b, i: (b * nblk + i, 0)),
        out_shape=jax.ShapeDtypeStruct((batch * seq, ATTN_WIDTH), BF16),
        scratch_shapes=[pltpu.VMEM((N_HEADS, 1, tq), F32), pltpu.VMEM((N_HEADS, 1, tq), F32),
                        pltpu.VMEM((N_HEADS, HEAD_DIM, tq), F32), pltpu.VMEM((N_HEADS, 1, tq), F32),
                        pltpu.VMEM((tq, N_HEADS), F32), pltpu.VMEM((N_HEADS, tq, tq), F32)],
        compiler_params=_params(("parallel", "arbitrary")),
        name="attn_prompt",
    )(qb, kb, vt, c, ct)


def _attn_sample_kernel(q_ref, kp_ref, vp_ref, kn_ref, vn_ref, cq_ref, ctp_ref, ctn_ref, o_ref, *, tq, past, scale):
    row = lax.broadcasted_iota(jnp.int32, (tq, tq), 0)
    col = lax.broadcasted_iota(jnp.int32, (tq, tq), 1)
    diag_mask = col <= row
    for h in range(N_HEADS):
        hs = slice(h * HEAD_DIM, (h + 1) * HEAD_DIM)
        q = q_ref[:, hs]
        cq = cq_ref[0, :, h:h + 1]
        carry = _softmax_init(tq)
        head_rows = pl.ds(h, past, stride=N_HEADS)
        carry = _softmax_step(carry, q, kp_ref[0, head_rows, :].astype(BF16), vp_ref[0, head_rows, :].astype(BF16),
                              cq, ctp_ref[0, h:h + 1, :], None, scale)
        _, l, acc = _softmax_step(carry, q, kn_ref[:, hs], vn_ref[:, hs],
                                  cq, ctn_ref[0, h:h + 1, :], diag_mask, scale)
        o_ref[:, hs] = (acc / l).astype(BF16)


def _attn_sample(qb, kb, vb, k_cache, v_cache, c_all, *, layer, batch, tq, past):
    cq = c_all[:, past:, :]
    ct = jnp.swapaxes(c_all, 1, 2)
    ct_past, ct_new = ct[:, :, :past], ct[:, :, past:]
    k_cache = k_cache.reshape(-1, past * N_HEADS, HEAD_DIM)
    v_cache = v_cache.reshape(-1, past * N_HEADS, HEAD_DIM)
    new = lambda: pl.BlockSpec((tq, ATTN_WIDTH), lambda b: (b, 0))
    old = lambda: pl.BlockSpec((1, past * N_HEADS, HEAD_DIM), lambda b: (layer * batch + b, 0, 0))
    return pl.pallas_call(
        functools.partial(_attn_sample_kernel, tq=tq, past=past, scale=HEAD_DIM ** -0.5),
        grid=(batch,),
        in_specs=[
            new(), old(), old(), new(), new(),
            pl.BlockSpec((1, tq, N_HEADS), lambda b: (b, 0, 0)),
            pl.BlockSpec((1, N_HEADS, past), lambda b: (b, 0, 0)),
            pl.BlockSpec((1, N_HEADS, tq), lambda b: (b, 0, 0)),
        ],
        out_specs=new(),
        out_shape=jax.ShapeDtypeStruct((batch * tq, ATTN_WIDTH), BF16),
        compiler_params=_params(("parallel",)),
        name="attn_sample",
    )(qb, k_cache, v_cache, kb, vb, cq, ct_past, ct_new)


def _pool_kernel(u_ref, prev_ref, d_ref, b0_ref, b1_ref, *, pos0):
    ts = u_ref.shape[1]
    pos = pos0 + lax.broadcasted_iota(jnp.int32, (ts, 1), 0)
    for g, d in enumerate(_pool_rows(prev_ref[0], u_ref[0], pos, b0_ref, b1_ref)):
        d_ref[0, :, g * POOL_GROUP:(g + 1) * POOL_GROUP] = d


def _pool(u, prev, *, pos0):
    b, n, c = u.shape
    return pl.pallas_call(
        functools.partial(_pool_kernel, pos0=pos0),
        grid=(b,),
        in_specs=[
            pl.BlockSpec((1, n, c), lambda i: (i, 0, 0)),
            pl.BlockSpec((1, HIST_ROWS, c), lambda i: (i, 0, 0)),
        ],
        out_specs=pl.BlockSpec((1, n, c), lambda i: (i, 0, 0)),
        out_shape=jax.ShapeDtypeStruct((b, n, c), BF16),
        scratch_shapes=[pltpu.VMEM((POOL_PAD + HIST_ROWS + n, c), F32)] * 2,
        compiler_params=_params(("parallel",)),
        name="pool",
    )(u, prev)


def _merge_kernel(x_ref, h_ref, o_ref, d_ref, wga_ref, wgb_ref, wat_ref, wpl_ref, ps_ref, wo_ref, y_ref):
    n = pl.program_id(1)

    @pl.when(n == 0)
    def _():
        y_ref[...] = x_ref[...]

    h = h_ref[...]
    gate_a = jax.nn.sigmoid(jnp.dot(h, wga_ref[...], preferred_element_type=F32))
    gate_b = jax.nn.sigmoid(jnp.dot(h, wgb_ref[...], preferred_element_type=F32))
    attn = jnp.dot(o_ref[...], wat_ref[...], preferred_element_type=F32)
    pool = jnp.dot(d_ref[...], wpl_ref[0], preferred_element_type=F32) * ps_ref[...]
    mixed = (gate_a * attn + gate_b * pool).astype(BF16)
    y_ref[...] += jnp.dot(mixed, wo_ref[...], preferred_element_type=F32)


def _merge(x, h, o, d, wga, wgb, wat, wpl, ps, wo, *, tm=512):
    t, dm = x.shape
    tn = POOL_OUT_GROUP
    return pl.pallas_call(
        _merge_kernel,
        grid=(t // tm, dm // tn),
        in_specs=[
            pl.BlockSpec((tm, dm), lambda i, n: (i, 0)),
            pl.BlockSpec((tm, dm), lambda i, n: (i, 0)),
            pl.BlockSpec((tm, ATTN_WIDTH), lambda i, n: (i, 0)),
            pl.BlockSpec((tm, POOL_GROUP), lambda i, n: (i, n)),
            pl.BlockSpec((dm, tn), lambda i, n: (0, n)),
            pl.BlockSpec((dm, tn), lambda i, n: (0, n)),
            pl.BlockSpec((ATTN_WIDTH, tn), lambda i, n: (0, n)),
            pl.BlockSpec((1, POOL_GROUP, tn), lambda i, n: (n, 0, 0)),
            pl.BlockSpec((1, tn), lambda i, n: (0, n)),
            pl.BlockSpec((tn, dm), lambda i, n: (n, 0)),
        ],
        out_specs=pl.BlockSpec((tm, dm), lambda i, n: (i, 0)),
        out_shape=jax.ShapeDtypeStruct((t, dm), F32),
        compiler_params=_params(("parallel", "arbitrary")),
        name="merge",
    )(x, h, o, d, wga, wgb, wat, wpl, ps, wo)


def _layer_weights(l, ffn1_norm, ffn1_w_gate, ffn1_w_up, ffn1_w_down, mix_norm, w_in, b_forget, w_branch_attn,
                   w_pool_group, pool_scale, w_out, ffn2_norm, ffn2_w_gate, ffn2_w_up, ffn2_w_down):
    a = ATTN_WIDTH
    wi = w_in[l]
    f0, u0 = 3 * a, 3 * a + N_HEADS
    ga0 = u0 + POOL_WIDTH
    gb0 = ga0 + D_MODEL
    wf = jnp.pad(wi[:, f0:u0], ((0, 0), (0, LANES - N_HEADS)))
    return dict(
        n1=ffn1_norm[l][None], g1=ffn1_w_gate[l].astype(BF16), u1=ffn1_w_up[l].astype(BF16),
        d1=(0.5 * ffn1_w_down[l]).astype(BF16),
        nm=mix_norm[l][None],
        wq=wi[:, :a].astype(BF16), wk=wi[:, a:2 * a].astype(BF16), wv=wi[:, 2 * a:3 * a].astype(BF16),
        wf=wf.astype(BF16), wu=wi[:, u0:ga0].astype(BF16),
        wga=wi[:, ga0:gb0].astype(BF16), wgb=wi[:, gb0:].astype(BF16),
        bf=b_forget[l][None],
        wat=w_branch_attn[l].astype(BF16), wpl=w_pool_group[l].astype(BF16), ps=pool_scale[l][None],
        wo=w_out[l].astype(BF16),
        n2=ffn2_norm[l][None], g2=ffn2_w_gate[l].astype(BF16), u2=ffn2_w_up[l].astype(BF16),
        d2=(0.5 * ffn2_w_down[l]).astype(BF16),
    )


def _layer(x, w, final_norm, *, batch, seq, past, last):
    x1 = _ffn(x, w["n1"], w["g1"], w["u1"], w["d1"], w["n1"], final_norm=False)
    proj = _inproj(x1, w["nm"], w["wq"], w["wk"], w["wv"], w["wu"], w["wf"], w["bf"],
                   seq=seq if past is None else None)
    qb, k, v, kb, vb, u, logf, hn = proj[:8]
    logf3 = logf.reshape(batch, seq, N_HEADS)
    u3 = u.reshape(batch, seq, POOL_WIDTH)
    if past is None:
        d, c = proj[8:]
        o = _attn_prompt(qb, kb, vb, c.reshape(batch, seq, N_HEADS), batch=batch, seq=seq)
        new_pool = u3[:, seq - POOL_HIST:, :]
    else:
        layer, k_cache, v_cache, logf_cache, hist = past
        plen = k_cache.shape[2]
        c_all = _cumsum_rows(logf_cache.reshape(-1, plen, N_HEADS), logf3, layer=layer)
        o = _attn_sample(qb, kb, vb, k_cache, v_cache, c_all, layer=layer, batch=batch, tq=seq, past=plen)
        hist_rows = jnp.pad(hist, ((0, 0), (HIST_ROWS - POOL_HIST, 0), (0, 0)))
        d = _pool(u3, hist_rows, pos0=plen).reshape(batch * seq, POOL_WIDTH)
        new_pool = jnp.concatenate([hist, u3], axis=1)[:, -POOL_HIST:, :]
    x2 = _merge(x1, hn, o, d, w["wga"], w["wgb"], w["wat"], w["wpl"], w["ps"], w["wo"])
    x3 = _ffn(x2, w["n2"], w["g2"], w["u2"], w["d2"], final_norm, final_norm=last)
    return (x3, k.reshape(batch, seq, N_HEADS, HEAD_DIM), v.reshape(batch, seq, N_HEADS, HEAD_DIM), logf3, new_pool)


def kernel(x_prompt, x_sample, cache_k, cache_v, cache_logf, state_pool, ffn1_norm, ffn1_w_gate, ffn1_w_up,
           ffn1_w_down, mix_norm, w_in, b_forget, w_branch_attn, w_pool_group, pool_scale, w_out, ffn2_norm,
           ffn2_w_gate, ffn2_w_up, ffn2_w_down, final_norm):
    depth = w_in.shape[0]
    bp, sp, _ = x_prompt.shape
    bs, ss, _ = x_sample.shape
    hp = x_prompt.reshape(bp * sp, D_MODEL)
    hs = x_sample.reshape(bs * ss, D_MODEL)
    fn = final_norm[None]
    outs = [[] for _ in range(8)]
    for l in range(depth):
        w = _layer_weights(l, ffn1_norm, ffn1_w_gate, ffn1_w_up, ffn1_w_down, mix_norm, w_in, b_forget,
                           w_branch_attn, w_pool_group, pool_scale, w_out, ffn2_norm, ffn2_w_gate, ffn2_w_up,
                           ffn2_w_down)
        last = l == depth - 1
        hp, k1, v1, l1, p1 = _layer(hp, w, fn, batch=bp, seq=sp, past=None, last=last)
        hs, k2, v2, l2, p2 = _layer(hs, w, fn, batch=bs, seq=ss,
                                    past=(l, cache_k, cache_v, cache_logf, state_pool[l]), last=last)
        for dst, val in zip(outs, (k1, v1, l1, p1, k2, v2, l2, p2)):
            dst.append(val)
    stacked = [jnp.stack(o) for o in outs]
    return (hp.reshape(bp, sp, D_MODEL), hs.reshape(bs, ss, D_MODEL), *stacked)
```

```python
import functools

import jax
import jax.numpy as jnp
from jax import lax
from jax.experimental import pallas as pl
from jax.experimental.pallas import tpu as pltpu

D_MODEL = 2048
N_HEADS = 8
HEAD_DIM = 128
ATTN_WIDTH = N_HEADS * HEAD_DIM
POOL_WINDOWS = (2, 4, 8, 16)
POOL_WIDTH = D_MODEL // 2
POOL_GROUP = POOL_WIDTH // len(POOL_WINDOWS)
POOL_OUT_GROUP = D_MODEL // len(POOL_WINDOWS)
POOL_HIST = max(POOL_WINDOWS) - 1
RMS_EPS = 1e-6
NEG_INF = -1e30
LOG2E = 1.4426950408889634

LANES = 128
HIST_ROWS = 16
POOL_PAD = 8
ATTN_BLOCK = 256
VMEM_LIMIT = 60 * 1024 * 1024

F32 = jnp.float32
BF16 = jnp.bfloat16


def _params(semantics):
    return pltpu.CompilerParams(dimension_semantics=semantics, vmem_limit_bytes=VMEM_LIMIT)


def _rmsnorm(x, g):
    r = lax.rsqrt(jnp.mean(x * x, axis=-1, keepdims=True) + RMS_EPS)
    return (x * r) * g


def _resident(shape):
    return pl.BlockSpec(shape, lambda *_: (0,) * len(shape), pipeline_mode=pl.Buffered(1))


def _ffn_kernel(x_ref, g_ref, wg_ref, wu_ref, wd_ref, fn_ref, o_ref, h_ref, *, final_norm):
    j = pl.program_id(1)

    @pl.when(j == 0)
    def _():
        x = x_ref[...]
        h_ref[...] = _rmsnorm(x, g_ref[...]).astype(BF16)
        o_ref[...] = x

    h = h_ref[...]
    gate = jnp.dot(h, wg_ref[...], preferred_element_type=F32)
    up = jnp.dot(h, wu_ref[...], preferred_element_type=F32)
    act = (gate * jax.nn.sigmoid(gate) * up).astype(BF16)
    o_ref[...] += jnp.dot(act, wd_ref[...], preferred_element_type=F32)

    if final_norm:
        @pl.when(j == pl.num_programs(1) - 1)
        def _():
            o_ref[...] = _rmsnorm(o_ref[...], fn_ref[...])


def _ffn(x, g, wg, wu, wd, fn, *, final_norm, tm=1024, tf=512):
    t, d = x.shape
    f = wg.shape[1]
    return pl.pallas_call(
        functools.partial(_ffn_kernel, final_norm=final_norm),
        grid=(t // tm, f // tf),
        in_specs=[
            pl.BlockSpec((tm, d), lambda i, j: (i, 0)),
            pl.BlockSpec((1, d), lambda i, j: (0, 0)),
            pl.BlockSpec((d, tf), lambda i, j: (0, j)),
            pl.BlockSpec((d, tf), lambda i, j: (0, j)),
            pl.BlockSpec((tf, d), lambda i, j: (j, 0)),
            pl.BlockSpec((1, d), lambda i, j: (0, 0)),
        ],
        out_specs=pl.BlockSpec((tm, d), lambda i, j: (i, 0)),
        out_shape=jax.ShapeDtypeStruct((t, d), F32),
        scratch_shapes=[pltpu.VMEM((tm, d), BF16)],
        compiler_params=_params(("parallel", "arbitrary")),
        name="ffn",
    )(x, g, wg, wu, wd, fn)


def _running_sum(x, buf_ref, pad):
    n = x.shape[0]
    shift = 1
    while shift < n:
        buf_ref[pl.ds(pad, n), :] = x
        x = x + buf_ref[pl.ds(pad - shift, n), :]
        shift *= 2
    return x


def _largest_shift(n):
    shift = 1
    while shift * 2 < n:
        shift *= 2
    return max(shift, 8)


def _pool_rows(prev, cur, pos, b0_ref, b1_ref):
    ts = cur.shape[0]
    n = HIST_ROWS + ts
    zero_rows = jnp.zeros((POOL_PAD, cur.shape[1]), F32)
    b0_ref[pl.ds(0, POOL_PAD), :] = zero_rows
    b1_ref[pl.ds(0, POOL_PAD), :] = zero_rows
    b0_ref[pl.ds(POOL_PAD, HIST_ROWS), :] = prev
    b0_ref[pl.ds(POOL_PAD + HIST_ROWS, ts), :] = cur
    out = []
    for g, w in enumerate(POOL_WINDOWS):
        gs = slice(g * POOL_GROUP, (g + 1) * POOL_GROUP)
        src, dst = b0_ref, b1_ref
        shift = 1
        while True:
            win = src[pl.ds(POOL_PAD, n), gs] + src[pl.ds(POOL_PAD - shift, n), gs]
            shift *= 2
            if shift == w:
                break
            dst[pl.ds(POOL_PAD, n), gs] = win
            src, dst = dst, src
        cnt = jnp.minimum(w, pos + 1).astype(F32)
        out.append((win[HIST_ROWS:] / cnt - cur[:, gs]).astype(BF16))
    return out


def _log_sigmoid(x):
    return jnp.minimum(x, 0.0) - jnp.log1p(jnp.exp(-jnp.abs(x)))


def _inproj_kernel(x_ref, g_ref, wq_ref, wk_ref, wv_ref, wu_ref, wf_ref, bf_ref,
                   qb_ref, k_ref, v_ref, kb_ref, vb_ref, u_ref, logf_ref, hn_ref, *rest, tiles_per_seq, csum_pad):
    if tiles_per_seq:
        d_ref, c_ref, hist_ref, total_ref, b0_ref, b1_ref, cbuf_ref = rest
        tile = pl.program_id(0) % tiles_per_seq

        @pl.when(tile == 0)
        def _():
            hist_ref[...] = jnp.zeros_like(hist_ref)
            total_ref[...] = jnp.zeros_like(total_ref)

    h = _rmsnorm(x_ref[...], g_ref[...]).astype(BF16)
    hn_ref[...] = h
    u = jnp.dot(h, wu_ref[...], preferred_element_type=F32)
    if tiles_per_seq:
        u_ref[0] = u[u.shape[0] - HIST_ROWS:, :]
    else:
        u_ref[...] = u
    f = jnp.dot(h, wf_ref[...], preferred_element_type=F32)
    logf = _log_sigmoid(f[:, :N_HEADS] + bf_ref[...])
    logf_ref[...] = logf
    if tiles_per_seq:
        tm = u.shape[0]
        pos = tile * tm + lax.broadcasted_iota(jnp.int32, (tm, 1), 0)
        for g, d in enumerate(_pool_rows(hist_ref[...], u, pos, b0_ref, b1_ref)):
            d_ref[:, g * POOL_GROUP:(g + 1) * POOL_GROUP] = d
        hist_ref[...] = u[tm - HIST_ROWS:, :]
        cbuf_ref[pl.ds(0, csum_pad), :] = jnp.zeros((csum_pad, N_HEADS), F32)
        c = _running_sum(logf, cbuf_ref, csum_pad) + total_ref[...]
        c_ref[...] = c
        total_ref[...] = c[tm - 1:, :]

    qb_ref[...] = jnp.dot(h, wq_ref[...], preferred_element_type=F32).astype(BF16)
    k = jnp.dot(h, wk_ref[...], preferred_element_type=F32)
    k_ref[...] = k
    kb_ref[...] = k.astype(BF16)
    v = jnp.dot(h, wv_ref[...], preferred_element_type=F32)
    v_ref[...] = v
    if tiles_per_seq:
        vb_ref[0] = v.T.astype(BF16)
    else:
        vb_ref[...] = v.astype(BF16)


def _inproj(x, g, wq, wk, wv, wu, wf, bforget, *, seq, tm=ATTN_BLOCK):
    t, d = x.shape
    assert t % tm == 0 and (seq is None or seq % tm == 0), (t, seq, tm)
    row = lambda width: pl.BlockSpec((tm, width), lambda i: (i, 0))
    out_specs = [row(ATTN_WIDTH), row(ATTN_WIDTH), row(ATTN_WIDTH), row(ATTN_WIDTH), row(ATTN_WIDTH),
                 row(POOL_WIDTH), row(N_HEADS), row(d)]
    out_shape = [
        jax.ShapeDtypeStruct((t, ATTN_WIDTH), BF16),
        jax.ShapeDtypeStruct((t, ATTN_WIDTH), F32),
        jax.ShapeDtypeStruct((t, ATTN_WIDTH), F32),
        jax.ShapeDtypeStruct((t, ATTN_WIDTH), BF16),
        jax.ShapeDtypeStruct((t, ATTN_WIDTH), BF16),
        jax.ShapeDtypeStruct((t, POOL_WIDTH), F32),
        jax.ShapeDtypeStruct((t, N_HEADS), F32),
        jax.ShapeDtypeStruct((t, d), BF16),
    ]
    scratch = []
    tiles_per_seq, csum_pad = 0, 0
    if seq is not None:
        tiles_per_seq, csum_pad = seq // tm, _largest_shift(tm)
        out_specs[4] = pl.BlockSpec((1, ATTN_WIDTH, tm), lambda i: (i, 0, 0))
        out_shape[4] = jax.ShapeDtypeStruct((t // tm, ATTN_WIDTH, tm), BF16)
        out_specs[5] = pl.BlockSpec((1, HIST_ROWS, POOL_WIDTH), lambda i: (i // tiles_per_seq, 0, 0))
        out_shape[5] = jax.ShapeDtypeStruct((t // seq, HIST_ROWS, POOL_WIDTH), F32)
        out_specs += [row(POOL_WIDTH), row(N_HEADS)]
        out_shape += [jax.ShapeDtypeStruct((t, POOL_WIDTH), BF16), jax.ShapeDtypeStruct((t, N_HEADS), F32)]
        scratch = [pltpu.VMEM((HIST_ROWS, POOL_WIDTH), F32), pltpu.VMEM((1, N_HEADS), F32),
                   pltpu.VMEM((POOL_PAD + HIST_ROWS + tm, POOL_WIDTH), F32),
                   pltpu.VMEM((POOL_PAD + HIST_ROWS + tm, POOL_WIDTH), F32),
                   pltpu.VMEM((csum_pad + tm, N_HEADS), F32)]
    return pl.pallas_call(
        functools.partial(_inproj_kernel, tiles_per_seq=tiles_per_seq, csum_pad=csum_pad),
        grid=(t // tm,),
        in_specs=[
            row(d),
            _resident((1, d)),
            _resident((d, ATTN_WIDTH)),
            _resident((d, ATTN_WIDTH)),
            _resident((d, ATTN_WIDTH)),
            _resident((d, POOL_WIDTH)),
            _resident((d, LANES)),
            _resident((1, N_HEADS)),
        ],
        out_specs=out_specs,
        out_shape=out_shape,
        scratch_shapes=scratch,
        compiler_params=_params(("arbitrary",)),
        name="inproj",
    )(x, g, wq, wk, wv, wu, wf, bforget)


def _cumsum_kernel(past_ref, new_ref, o_ref, buf_ref, *, pad):
    buf_ref[pl.ds(0, pad), :] = jnp.zeros((pad, new_ref.shape[2]), F32)
    rows = jnp.concatenate([past_ref[0].astype(F32), new_ref[0]], axis=0)
    o_ref[0] = _running_sum(rows, buf_ref, pad)


def _cumsum_rows(past, new, *, layer):
    b, t, h = new.shape
    p = past.shape[1]
    n = p + t
    pad = _largest_shift(n)
    return pl.pallas_call(
        functools.partial(_cumsum_kernel, pad=pad),
        grid=(b,),
        in_specs=[pl.BlockSpec((1, p, h), lambda i: (layer * b + i, 0, 0)),
                  pl.BlockSpec((1, t, h), lambda i: (i, 0, 0))],
        out_specs=pl.BlockSpec((1, n, h), lambda i: (i, 0, 0)),
        out_shape=jax.ShapeDtypeStruct((b, n, h), F32),
        scratch_shapes=[pltpu.VMEM((pad + n, h), F32)],
        compiler_params=_params(("parallel",)),
        name="cumsum",
    )(past, new)


def _softmax_step(carry, q, k, v, cq, ck, mask, scale):
    m, l, acc = carry
    s = lax.dot_general(q, k, (((1,), (1,)), ((), ())), preferred_element_type=F32) * scale
    logit = s + (cq - ck)
    if mask is not None:
        logit = jnp.where(mask, logit, NEG_INF)
    m_new = jnp.maximum(m, jnp.max(logit, axis=1, keepdims=True))
    alpha = jnp.exp(m - m_new)
    p = jnp.exp(logit - m_new)
    l = alpha * l + jnp.sum(p, axis=1, keepdims=True)
    acc = alpha * acc + jnp.dot(p.astype(BF16), v, preferred_element_type=F32)
    return m_new, l, acc


def _softmax_init(tq):
    return (jnp.full((tq, 1), NEG_INF, F32), jnp.zeros((tq, 1), F32), jnp.zeros((tq, HEAD_DIM), F32))


def _attn_prompt_kernel(q_ref, k_ref, vt_ref, c_ref, ct_ref, o_ref, m_ref, l_ref, acc_ref, cq_ref,
                        cs_ref, a_ref, *, tq, nblk, scale):
    qi = pl.program_id(1)
    scale2 = scale * LOG2E
    key = lax.broadcasted_iota(jnp.int32, (tq, tq), 0)
    qry = lax.broadcasted_iota(jnp.int32, (tq, tq), 1)
    diag_mask = key <= qry
    for h in range(N_HEADS):
        m_ref[h] = jnp.full((1, tq), NEG_INF, F32)
        l_ref[h] = jnp.zeros((1, tq), F32)
        acc_ref[h] = jnp.zeros((HEAD_DIM, tq), F32)
        cq_ref[h] = ct_ref[0, pl.ds(h * nblk + qi, 1), :] * LOG2E

    def block(kb, mask):
        k0 = pl.multiple_of(kb * tq, tq)
        cs_ref[...] = c_ref[0, pl.ds(k0, tq), :] * LOG2E
        col_max = []
        for h in range(N_HEADS):
            hs = slice(h * HEAD_DIM, (h + 1) * HEAD_DIM)
            st = lax.dot_general(k_ref[pl.ds(k0, tq), hs], q_ref[:, hs], (((1,), (1,)), ((), ())),
                                 preferred_element_type=F32)
            a = st * scale2 - cs_ref[:, h:h + 1]
            if mask is not None:
                a = jnp.where(mask, a, NEG_INF)
            a_ref[h] = a
            col_max.append(jnp.max(a, axis=0, keepdims=True))
        for h in range(N_HEADS):
            hs = slice(h * HEAD_DIM, (h + 1) * HEAD_DIM)
            m = m_ref[h]
            m_new = jnp.maximum(m, col_max[h] + cq_ref[h])
            alpha = jnp.exp2(m - m_new)
            p = jnp.exp2(a_ref[h] - (m_new - cq_ref[h]))
            l_ref[h] = alpha * l_ref[h] + jnp.sum(p, axis=0, keepdims=True)
            acc_ref[h] = alpha * acc_ref[h] + jnp.dot(vt_ref[kb, hs, :], p.astype(BF16),
                                                      preferred_element_type=F32)
            m_ref[h] = m_new

    def body(kb, carry):
        block(kb, None)
        return carry

    lax.fori_loop(0, qi, body, 0)
    block(qi, diag_mask)
    for h in range(N_HEADS):
        hs = slice(h * HEAD_DIM, (h + 1) * HEAD_DIM)
        o_ref[:, hs] = (acc_ref[h] / l_ref[h]).T.astype(BF16)


def _attn_prompt(qb, kb, vt, c, *, batch, seq, tq=ATTN_BLOCK):
    nblk = seq // tq
    ct = jnp.swapaxes(c, 1, 2).reshape(batch, N_HEADS * nblk, tq)
    return pl.pallas_call(
        functools.partial(_attn_prompt_kernel, tq=tq, nblk=nblk, scale=HEAD_DIM ** -0.5),
        grid=(batch, nblk),
        in_specs=[
            pl.BlockSpec((tq, ATTN_WIDTH), lambda b, i: (b * nblk + i, 0)),
            pl.BlockSpec((seq, ATTN_WIDTH), lambda b, i: (b, 0)),
            pl.BlockSpec((nblk, ATTN_WIDTH, tq), lambda b, i: (b, 0, 0)),
            pl.BlockSpec((1, seq, N_HEADS), lambda b, i: (b, 0, 0)),
            pl.BlockSpec((1, N_HEADS * nblk, tq), lambda b, i: (b, 0, 0)),
        ],
        out_specs=pl.BlockSpec((tq, ATTN_WIDTH), lambda b, i: (b * nblk + i, 0)),
        out_shape=jax.ShapeDtypeStruct((batch * seq, ATTN_WIDTH), BF16),
        scratch_shapes=[pltpu.VMEM((N_HEADS, 1, tq), F32), pltpu.VMEM((N_HEADS, 1, tq), F32),
                        pltpu.VMEM((N_HEADS, HEAD_DIM, tq), F32), pltpu.VMEM((N_HEADS, 1, tq), F32),
                        pltpu.VMEM((tq, N_HEADS), F32), pltpu.VMEM((N_HEADS, tq, tq), F32)],
        compiler_params=_params(("parallel", "arbitrary")),
        name="attn_prompt",
    )(qb, kb, vt, c, ct)


def _attn_sample_kernel(q_ref, kp_ref, vp_ref, kn_ref, vn_ref, cq_ref, ctp_ref, ctn_ref, o_ref, *, tq, past, scale):
    row = lax.broadcasted_iota(jnp.int32, (tq, tq), 0)
    col = lax.broadcasted_iota(jnp.int32, (tq, tq), 1)
    diag_mask = col <= row
    for h in range(N_HEADS):
        hs = slice(h * HEAD_DIM, (h + 1) * HEAD_DIM)
        q = q_ref[:, hs]
        cq = cq_ref[0, :, h:h + 1]
        carry = _softmax_init(tq)
        head_rows = pl.ds(h, past, stride=N_HEADS)
        carry = _softmax_step(carry, q, kp_ref[0, head_rows, :].astype(BF16), vp_ref[0, head_rows, :].astype(BF16),
                              cq, ctp_ref[0, h:h + 1, :], None, scale)
        _, l, acc = _softmax_step(carry, q, kn_ref[:, hs], vn_ref[:, hs],
                                  cq, ctn_ref[0, h:h + 1, :], diag_mask, scale)
        o_ref[:, hs] = (acc / l).astype(BF16)


def _attn_sample(qb, kb, vb, k_cache, v_cache, c_all, *, layer, batch, tq, past):
    cq = c_all[:, past:, :]
    ct = jnp.swapaxes(c_all, 1, 2)
    ct_past, ct_new = ct[:, :, :past], ct[:, :, past:]
    k_cache = k_cache.reshape(-1, past * N_HEADS, HEAD_DIM)
    v_cache = v_cache.reshape(-1, past * N_HEADS, HEAD_DIM)
    new = lambda: pl.BlockSpec((tq, ATTN_WIDTH), lambda b: (b, 0))
    old = lambda: pl.BlockSpec((1, past * N_HEADS, HEAD_DIM), lambda b: (layer * batch + b, 0, 0))
    return pl.pallas_call(
        functools.partial(_attn_sample_kernel, tq=tq, past=past, scale=HEAD_DIM ** -0.5),
        grid=(batch,),
        in_specs=[
            new(), old(), old(), new(), new(),
            pl.BlockSpec((1, tq, N_HEADS), lambda b: (b, 0, 0)),
            pl.BlockSpec((1, N_HEADS, past), lambda b: (b, 0, 0)),
            pl.BlockSpec((1, N_HEADS, tq), lambda b: (b, 0, 0)),
        ],
        out_specs=new(),
        out_shape=jax.ShapeDtypeStruct((batch * tq, ATTN_WIDTH), BF16),
        compiler_params=_params(("parallel",)),
        name="attn_sample",
    )(qb, k_cache, v_cache, kb, vb, cq, ct_past, ct_new)


def _pool_kernel(u_ref, prev_ref, d_ref, b0_ref, b1_ref, *, pos0):
    ts = u_ref.shape[1]
    pos = pos0 + lax.broadcasted_iota(jnp.int32, (ts, 1), 0)
    for g, d in enumerate(_pool_rows(prev_ref[0], u_ref[0], pos, b0_ref, b1_ref)):
        d_ref[0, :, g * POOL_GROUP:(g + 1) * POOL_GROUP] = d


def _pool(u, prev, *, pos0):
    b, n, c = u.shape
    return pl.pallas_call(
        functools.partial(_pool_kernel, pos0=pos0),
        grid=(b,),
        in_specs=[
            pl.BlockSpec((1, n, c), lambda i: (i, 0, 0)),
            pl.BlockSpec((1, HIST_ROWS, c), lambda i: (i, 0, 0)),
        ],
        out_specs=pl.BlockSpec((1, n, c), lambda i: (i, 0, 0)),
        out_shape=jax.ShapeDtypeStruct((b, n, c), BF16),
        scratch_shapes=[pltpu.VMEM((POOL_PAD + HIST_ROWS + n, c), F32)] * 2,
        compiler_params=_params(("parallel",)),
        name="pool",
    )(u, prev)


def _merge_kernel(x_ref, h_ref, o_ref, d_ref, wga_ref, wgb_ref, wat_ref, wpl_ref, ps_ref, wo_ref, y_ref):
    h = h_ref[...]
    o = o_ref[...]
    y = x_ref[...]
    for n in range(len(POOL_WINDOWS)):
        cols = slice(n * POOL_OUT_GROUP, (n + 1) * POOL_OUT_GROUP)
        gate_a = jax.nn.sigmoid(jnp.dot(h, wga_ref[:, cols], preferred_element_type=F32))
        gate_b = jax.nn.sigmoid(jnp.dot(h, wgb_ref[:, cols], preferred_element_type=F32))
        attn = jnp.dot(o, wat_ref[:, cols], preferred_element_type=F32)
        pool = jnp.dot(d_ref[:, n * POOL_GROUP:(n + 1) * POOL_GROUP], wpl_ref[n],
                       preferred_element_type=F32) * ps_ref[:, cols]
        mixed = (gate_a * attn + gate_b * pool).astype(BF16)
        y = y + jnp.dot(mixed, wo_ref[cols, :], preferred_element_type=F32)
    y_ref[...] = y


def _merge(x, h, o, d, wga, wgb, wat, wpl, ps, wo, *, tm=256):
    t, dm = x.shape
    row = lambda width: pl.BlockSpec((tm, width), lambda i: (i, 0))
    return pl.pallas_call(
        _merge_kernel,
        grid=(t // tm,),
        in_specs=[row(dm), row(dm), row(ATTN_WIDTH), row(POOL_WIDTH),
                  _resident(wga.shape), _resident(wgb.shape), _resident(wat.shape), _resident(wpl.shape),
                  _resident(ps.shape), _resident(wo.shape)],
        out_specs=row(dm),
        out_shape=jax.ShapeDtypeStruct((t, dm), F32),
        compiler_params=_params(("parallel",)),
        name="merge",
    )(x, h, o, d, wga, wgb, wat, wpl, ps, wo)


def _layer_weights(l, ffn1_norm, ffn1_w_gate, ffn1_w_up, ffn1_w_down, mix_norm, w_in, b_forget, w_branch_attn,
                   w_pool_group, pool_scale, w_out, ffn2_norm, ffn2_w_gate, ffn2_w_up, ffn2_w_down):
    a = ATTN_WIDTH
    wi = w_in[l]
    f0, u0 = 3 * a, 3 * a + N_HEADS
    ga0 = u0 + POOL_WIDTH
    gb0 = ga0 + D_MODEL
    wf = jnp.pad(wi[:, f0:u0], ((0, 0), (0, LANES - N_HEADS)))
    return dict(
        n1=ffn1_norm[l][None], g1=ffn1_w_gate[l].astype(BF16), u1=ffn1_w_up[l].astype(BF16),
        d1=(0.5 * ffn1_w_down[l]).astype(BF16),
        nm=mix_norm[l][None],
        wq=wi[:, :a].astype(BF16), wk=wi[:, a:2 * a].astype(BF16), wv=wi[:, 2 * a:3 * a].astype(BF16),
        wf=wf.astype(BF16), wu=wi[:, u0:ga0].astype(BF16),
        wga=wi[:, ga0:gb0].astype(BF16), wgb=wi[:, gb0:].astype(BF16),
        bf=b_forget[l][None],
        wat=w_branch_attn[l].astype(BF16), wpl=w_pool_group[l].astype(BF16), ps=pool_scale[l][None],
        wo=w_out[l].astype(BF16),
        n2=ffn2_norm[l][None], g2=ffn2_w_gate[l].astype(BF16), u2=ffn2_w_up[l].astype(BF16),
        d2=(0.5 * ffn2_w_down[l]).astype(BF16),
    )


def _layer(x, w, final_norm, *, batch, seq, past, last):
    x1 = _ffn(x, w["n1"], w["g1"], w["u1"], w["d1"], w["n1"], final_norm=False)
    proj = _inproj(x1, w["nm"], w["wq"], w["wk"], w["wv"], w["wu"], w["wf"], w["bf"],
                   seq=seq if past is None else None)
    qb, k, v, kb, vb, u, logf, hn = proj[:8]
    logf3 = logf.reshape(batch, seq, N_HEADS)
    if past is None:
        d, c = proj[8:]
        o = _attn_prompt(qb, kb, vb, c.reshape(batch, seq, N_HEADS), batch=batch, seq=seq)
        new_pool = u[:, HIST_ROWS - POOL_HIST:, :]
    else:
        layer, k_cache, v_cache, logf_cache, hist = past
        u3 = u.reshape(batch, seq, POOL_WIDTH)
        plen = k_cache.shape[2]
        c_all = _cumsum_rows(logf_cache.reshape(-1, plen, N_HEADS), logf3, layer=layer)
        o = _attn_sample(qb, kb, vb, k_cache, v_cache, c_all, layer=layer, batch=batch, tq=seq, past=plen)
        hist_rows = jnp.pad(hist, ((0, 0), (HIST_ROWS - POOL_HIST, 0), (0, 0)))
        d = _pool(u3, hist_rows, pos0=plen).reshape(batch * seq, POOL_WIDTH)
        new_pool = jnp.concatenate([hist, u3], axis=1)[:, -POOL_HIST:, :]
    x2 = _merge(x1, hn, o, d, w["wga"], w["wgb"], w["wat"], w["wpl"], w["ps"], w["wo"])
    x3 = _ffn(x2, w["n2"], w["g2"], w["u2"], w["d2"], final_norm, final_norm=last)
    return (x3, k.reshape(batch, seq, N_HEADS, HEAD_DIM), v.reshape(batch, seq, N_HEADS, HEAD_DIM), logf3, new_pool)


def kernel(x_prompt, x_sample, cache_k, cache_v, cache_logf, state_pool, ffn1_norm, ffn1_w_gate, ffn1_w_up,
           ffn1_w_down, mix_norm, w_in, b_forget, w_branch_attn, w_pool_group, pool_scale, w_out, ffn2_norm,
           ffn2_w_gate, ffn2_w_up, ffn2_w_down, final_norm):
    depth = w_in.shape[0]
    bp, sp, _ = x_prompt.shape
    bs, ss, _ = x_sample.shape
    hp = x_prompt.reshape(bp * sp, D_MODEL)
    hs = x_sample.reshape(bs * ss, D_MODEL)
    fn = final_norm[None]
    outs = [[] for _ in range(8)]
    for l in range(depth):
        w = _layer_weights(l, ffn1_norm, ffn1_w_gate, ffn1_w_up, ffn1_w_down, mix_norm, w_in, b_forget,
                           w_branch_attn, w_pool_group, pool_scale, w_out, ffn2_norm, ffn2_w_gate, ffn2_w_up,
                           ffn2_w_down)
        last = l == depth - 1
        hp, k1, v1, l1, p1 = _layer(hp, w, fn, batch=bp, seq=sp, past=None, last=last)
        hs, k2, v2, l2, p2 = _layer(hs, w, fn, batch=bs, seq=ss,
                                    past=(l, cache_k, cache_v, cache_logf, state_pool[l]), last=last)
        for dst, val in zip(outs, (k1, v1, l1, p1, k2, v2, l2, p2)):
            dst.append(val)
    stacked = [jnp.stack(o) for o in outs]
    return (hp.reshape(bp, sp, D_MODEL), hs.reshape(bs, ss, D_MODEL), *stacked)
```

```python
import functools

import jax
import jax.numpy as jnp
from jax import lax
from jax.experimental import pallas as pl
from jax.experimental.pallas import tpu as pltpu

D_MODEL = 2048
N_HEADS = 8
HEAD_DIM = 128
ATTN_WIDTH = N_HEADS * HEAD_DIM
POOL_WINDOWS = (2, 4, 8, 16)
POOL_WIDTH = D_MODEL // 2
POOL_GROUP = POOL_WIDTH // len(POOL_WINDOWS)
POOL_OUT_GROUP = D_MODEL // len(POOL_WINDOWS)
POOL_HIST = max(POOL_WINDOWS) - 1
RMS_EPS = 1e-6
NEG_INF = -1e30
LOG2E = 1.4426950408889634

LANES = 128
HIST_ROWS = 16
POOL_PAD = 8
ATTN_BLOCK = 256
VMEM_LIMIT = 60 * 1024 * 1024

F32 = jnp.float32
BF16 = jnp.bfloat16


def _params(semantics):
    return pltpu.CompilerParams(dimension_semantics=semantics, vmem_limit_bytes=VMEM_LIMIT)


def _rmsnorm(x, g):
    r = lax.rsqrt(jnp.mean(x * x, axis=-1, keepdims=True) + RMS_EPS)
    return (x * r) * g


def _resident(shape):
    return pl.BlockSpec(shape, lambda *_: (0,) * len(shape), pipeline_mode=pl.Buffered(1))


def _ffn_kernel(x_ref, g_ref, wg_ref, wu_ref, wd_ref, fn_ref, o_ref, h_ref, *, final_norm):
    j = pl.program_id(1)

    @pl.when(j == 0)
    def _():
        x = x_ref[...]
        h_ref[...] = _rmsnorm(x, g_ref[...]).astype(BF16)
        o_ref[...] = x

    h = h_ref[...]
    gate = jnp.dot(h, wg_ref[...], preferred_element_type=F32)
    up = jnp.dot(h, wu_ref[...], preferred_element_type=F32)
    act = (gate * jax.nn.sigmoid(gate) * up).astype(BF16)
    o_ref[...] += jnp.dot(act, wd_ref[...], preferred_element_type=F32)

    if final_norm:
        @pl.when(j == pl.num_programs(1) - 1)
        def _():
            o_ref[...] = _rmsnorm(o_ref[...], fn_ref[...])


def _ffn(x, g, wg, wu, wd, fn, *, final_norm, tm=1024, tf=512):
    t, d = x.shape
    f = wg.shape[1]
    return pl.pallas_call(
        functools.partial(_ffn_kernel, final_norm=final_norm),
        grid=(t // tm, f // tf),
        in_specs=[
            pl.BlockSpec((tm, d), lambda i, j: (i, 0)),
            pl.BlockSpec((1, d), lambda i, j: (0, 0)),
            pl.BlockSpec((d, tf), lambda i, j: (0, j)),
            pl.BlockSpec((d, tf), lambda i, j: (0, j)),
            pl.BlockSpec((tf, d), lambda i, j: (j, 0)),
            pl.BlockSpec((1, d), lambda i, j: (0, 0)),
        ],
        out_specs=pl.BlockSpec((tm, d), lambda i, j: (i, 0)),
        out_shape=jax.ShapeDtypeStruct((t, d), F32),
        scratch_shapes=[pltpu.VMEM((tm, d), BF16)],
        compiler_params=_params(("parallel", "arbitrary")),
        name="ffn",
    )(x, g, wg, wu, wd, fn)


def _running_sum(x, buf_ref, pad):
    n = x.shape[0]
    shift = 1
    while shift < n:
        buf_ref[pl.ds(pad, n), :] = x
        x = x + buf_ref[pl.ds(pad - shift, n), :]
        shift *= 2
    return x


def _largest_shift(n):
    shift = 1
    while shift * 2 < n:
        shift *= 2
    return max(shift, 8)


def _pool_rows(prev, cur, pos, b0_ref, b1_ref):
    ts = cur.shape[0]
    n = HIST_ROWS + ts
    zero_rows = jnp.zeros((POOL_PAD, cur.shape[1]), F32)
    b0_ref[pl.ds(0, POOL_PAD), :] = zero_rows
    b1_ref[pl.ds(0, POOL_PAD), :] = zero_rows
    b0_ref[pl.ds(POOL_PAD, HIST_ROWS), :] = prev
    b0_ref[pl.ds(POOL_PAD + HIST_ROWS, ts), :] = cur
    out = []
    for g, w in enumerate(POOL_WINDOWS):
        gs = slice(g * POOL_GROUP, (g + 1) * POOL_GROUP)
        src, dst = b0_ref, b1_ref
        shift = 1
        while True:
            win = src[pl.ds(POOL_PAD, n), gs] + src[pl.ds(POOL_PAD - shift, n), gs]
            shift *= 2
            if shift == w:
                break
            dst[pl.ds(POOL_PAD, n), gs] = win
            src, dst = dst, src
        cnt = jnp.minimum(w, pos + 1).astype(F32)
        out.append((win[HIST_ROWS:] / cnt - cur[:, gs]).astype(BF16))
    return out


def _log_sigmoid(x):
    return jnp.minimum(x, 0.0) - jnp.log1p(jnp.exp(-jnp.abs(x)))


def _inproj_kernel(x_ref, g_ref, wq_ref, wk_ref, wv_ref, wu_ref, wf_ref, bf_ref,
                   qb_ref, k_ref, v_ref, kb_ref, vb_ref, u_ref, logf_ref, hn_ref, *rest, tiles_per_seq, csum_pad):
    if tiles_per_seq:
        d_ref, c_ref, ct_ref, hist_ref, total_ref, b0_ref, b1_ref, cbuf_ref = rest
        tile = pl.program_id(0) % tiles_per_seq

        @pl.when(tile == 0)
        def _():
            hist_ref[...] = jnp.zeros_like(hist_ref)
            total_ref[...] = jnp.zeros_like(total_ref)

    h = _rmsnorm(x_ref[...], g_ref[...]).astype(BF16)
    hn_ref[...] = h
    u = jnp.dot(h, wu_ref[...], preferred_element_type=F32)
    if tiles_per_seq:
        u_ref[0] = u[u.shape[0] - HIST_ROWS:, :]
    else:
        u_ref[...] = u
    f = jnp.dot(h, wf_ref[...], preferred_element_type=F32)
    logf = _log_sigmoid(f[:, :N_HEADS] + bf_ref[...])
    if tiles_per_seq:
        logf_ref[0] = logf.T
        tm = u.shape[0]
        pos = tile * tm + lax.broadcasted_iota(jnp.int32, (tm, 1), 0)
        for g, d in enumerate(_pool_rows(hist_ref[...], u, pos, b0_ref, b1_ref)):
            d_ref[:, g * POOL_GROUP:(g + 1) * POOL_GROUP] = d
        hist_ref[...] = u[tm - HIST_ROWS:, :]
        cbuf_ref[pl.ds(0, csum_pad), :] = jnp.zeros((csum_pad, N_HEADS), F32)
        c = _running_sum(logf, cbuf_ref, csum_pad) + total_ref[...]
        c_ref[...] = c
        ct_ref[0] = c.T
        total_ref[...] = c[tm - 1:, :]
    else:
        logf_ref[...] = logf

    qb_ref[...] = jnp.dot(h, wq_ref[...], preferred_element_type=F32).astype(BF16)
    k = jnp.dot(h, wk_ref[...], preferred_element_type=F32)
    k_ref[...] = k
    kb_ref[...] = k.astype(BF16)
    v = jnp.dot(h, wv_ref[...], preferred_element_type=F32)
    v_ref[...] = v
    if tiles_per_seq:
        vb_ref[0] = v.T.astype(BF16)
    else:
        vb_ref[...] = v.astype(BF16)


def _inproj(x, g, wq, wk, wv, wu, wf, bforget, *, seq, tm=ATTN_BLOCK):
    t, d = x.shape
    assert t % tm == 0 and (seq is None or seq % tm == 0), (t, seq, tm)
    row = lambda width: pl.BlockSpec((tm, width), lambda i: (i, 0))
    out_specs = [row(ATTN_WIDTH), row(ATTN_WIDTH), row(ATTN_WIDTH), row(ATTN_WIDTH), row(ATTN_WIDTH),
                 row(POOL_WIDTH), row(N_HEADS), row(d)]
    out_shape = [
        jax.ShapeDtypeStruct((t, ATTN_WIDTH), BF16),
        jax.ShapeDtypeStruct((t, ATTN_WIDTH), F32),
        jax.ShapeDtypeStruct((t, ATTN_WIDTH), F32),
        jax.ShapeDtypeStruct((t, ATTN_WIDTH), BF16),
        jax.ShapeDtypeStruct((t, ATTN_WIDTH), BF16),
        jax.ShapeDtypeStruct((t, POOL_WIDTH), F32),
        jax.ShapeDtypeStruct((t, N_HEADS), F32),
        jax.ShapeDtypeStruct((t, d), BF16),
    ]
    scratch = []
    tiles_per_seq, csum_pad = 0, 0
    if seq is not None:
        tiles_per_seq, csum_pad = seq // tm, _largest_shift(tm)
        time_minor = pl.BlockSpec((1, N_HEADS, tm), lambda i: (i // tiles_per_seq, 0, i % tiles_per_seq))
        out_specs[4] = pl.BlockSpec((1, ATTN_WIDTH, tm), lambda i: (i, 0, 0))
        out_shape[4] = jax.ShapeDtypeStruct((t // tm, ATTN_WIDTH, tm), BF16)
        out_specs[5] = pl.BlockSpec((1, HIST_ROWS, POOL_WIDTH), lambda i: (i // tiles_per_seq, 0, 0))
        out_shape[5] = jax.ShapeDtypeStruct((t // seq, HIST_ROWS, POOL_WIDTH), F32)
        out_specs[6] = time_minor
        out_shape[6] = jax.ShapeDtypeStruct((t // seq, N_HEADS, seq), F32)
        out_specs += [row(POOL_WIDTH), row(N_HEADS), time_minor]
        out_shape += [jax.ShapeDtypeStruct((t, POOL_WIDTH), BF16), jax.ShapeDtypeStruct((t, N_HEADS), F32),
                      jax.ShapeDtypeStruct((t // seq, N_HEADS, seq), F32)]
        scratch = [pltpu.VMEM((HIST_ROWS, POOL_WIDTH), F32), pltpu.VMEM((1, N_HEADS), F32),
                   pltpu.VMEM((POOL_PAD + HIST_ROWS + tm, POOL_WIDTH), F32),
                   pltpu.VMEM((POOL_PAD + HIST_ROWS + tm, POOL_WIDTH), F32),
                   pltpu.VMEM((csum_pad + tm, N_HEADS), F32)]
    return pl.pallas_call(
        functools.partial(_inproj_kernel, tiles_per_seq=tiles_per_seq, csum_pad=csum_pad),
        grid=(t // tm,),
        in_specs=[
            row(d),
            _resident((1, d)),
            _resident((d, ATTN_WIDTH)),
            _resident((d, ATTN_WIDTH)),
            _resident((d, ATTN_WIDTH)),
            _resident((d, POOL_WIDTH)),
            _resident((d, LANES)),
            _resident((1, N_HEADS)),
        ],
        out_specs=out_specs,
        out_shape=out_shape,
        scratch_shapes=scratch,
        compiler_params=_params(("arbitrary",)),
        name="inproj",
    )(x, g, wq, wk, wv, wu, wf, bforget)


def _cumsum_kernel(past_ref, new_ref, ctp_ref, ctn_ref, cq_ref, lt_ref, buf_ref, *, pad):
    p = past_ref.shape[2]
    new = new_ref[0]
    buf_ref[pl.ds(0, pad), :] = jnp.zeros((pad, new.shape[1]), F32)
    rows = jnp.concatenate([past_ref[0].astype(F32).T, new], axis=0)
    c = _running_sum(rows, buf_ref, pad)
    ctp_ref[0] = c[:p].T
    ctn_ref[0] = c[p:].T
    cq_ref[0] = c[p:]
    lt_ref[0] = new.T


def _cumsum_rows(past, new, *, layer):
    b, t, h = new.shape
    p = past.shape[2]
    n = p + t
    pad = _largest_shift(n)
    per_stream = lambda *shape: pl.BlockSpec((1,) + shape, lambda i: (i, 0, 0))
    return pl.pallas_call(
        functools.partial(_cumsum_kernel, pad=pad),
        grid=(b,),
        in_specs=[pl.BlockSpec((1, h, p), lambda i: (layer * b + i, 0, 0)), per_stream(t, h)],
        out_specs=[per_stream(h, p), per_stream(h, t), per_stream(t, h), per_stream(h, t)],
        out_shape=[jax.ShapeDtypeStruct((b, h, p), F32), jax.ShapeDtypeStruct((b, h, t), F32),
                   jax.ShapeDtypeStruct((b, t, h), F32), jax.ShapeDtypeStruct((b, h, t), F32)],
        scratch_shapes=[pltpu.VMEM((pad + n, h), F32)],
        compiler_params=_params(("parallel",)),
        name="cumsum",
    )(past, new)


def _softmax_step(carry, q, k, v, cq, ck, mask, scale):
    m, l, acc = carry
    s = lax.dot_general(q, k, (((1,), (1,)), ((), ())), preferred_element_type=F32) * scale
    logit = s + (cq - ck)
    if mask is not None:
        logit = jnp.where(mask, logit, NEG_INF)
    m_new = jnp.maximum(m, jnp.max(logit, axis=1, keepdims=True))
    alpha = jnp.exp(m - m_new)
    p = jnp.exp(logit - m_new)
    l = alpha * l + jnp.sum(p, axis=1, keepdims=True)
    acc = alpha * acc + jnp.dot(p.astype(BF16), v, preferred_element_type=F32)
    return m_new, l, acc


def _softmax_init(tq):
    return (jnp.full((tq, 1), NEG_INF, F32), jnp.zeros((tq, 1), F32), jnp.zeros((tq, HEAD_DIM), F32))


def _attn_prompt_kernel(q_ref, k_ref, vt_ref, c_ref, ct_ref, o_ref, m_ref, l_ref, acc_ref, cq_ref,
                        cs_ref, a_ref, *, tq, nblk, scale):
    qi = pl.program_id(1)
    scale2 = scale * LOG2E
    key = lax.broadcasted_iota(jnp.int32, (tq, tq), 0)
    qry = lax.broadcasted_iota(jnp.int32, (tq, tq), 1)
    diag_mask = key <= qry
    for h in range(N_HEADS):
        m_ref[h] = jnp.full((1, tq), NEG_INF, F32)
        l_ref[h] = jnp.zeros((1, tq), F32)
        acc_ref[h] = jnp.zeros((HEAD_DIM, tq), F32)
        cq_ref[h] = ct_ref[0, pl.ds(h * nblk + qi, 1), :] * LOG2E

    def block(kb, mask):
        k0 = pl.multiple_of(kb * tq, tq)
        cs_ref[...] = c_ref[0, pl.ds(k0, tq), :] * LOG2E
        col_max = []
        for h in range(N_HEADS):
            hs = slice(h * HEAD_DIM, (h + 1) * HEAD_DIM)
            st = lax.dot_general(k_ref[pl.ds(k0, tq), hs], q_ref[:, hs], (((1,), (1,)), ((), ())),
                                 preferred_element_type=F32)
            a = st * scale2 - cs_ref[:, h:h + 1]
            if mask is not None:
                a = jnp.where(mask, a, NEG_INF)
            a_ref[h] = a
            col_max.append(jnp.max(a, axis=0, keepdims=True))
        for h in range(N_HEADS):
            hs = slice(h * HEAD_DIM, (h + 1) * HEAD_DIM)
            m = m_ref[h]
            m_new = jnp.maximum(m, col_max[h] + cq_ref[h])
            alpha = jnp.exp2(m - m_new)
            p = jnp.exp2(a_ref[h] - (m_new - cq_ref[h]))
            l_ref[h] = alpha * l_ref[h] + jnp.sum(p, axis=0, keepdims=True)
            acc_ref[h] = alpha * acc_ref[h] + jnp.dot(vt_ref[kb, hs, :], p.astype(BF16),
                                                      preferred_element_type=F32)
            m_ref[h] = m_new

    def body(kb, carry):
        block(kb, None)
        return carry

    lax.fori_loop(0, qi, body, 0)
    block(qi, diag_mask)
    for h in range(N_HEADS):
        hs = slice(h * HEAD_DIM, (h + 1) * HEAD_DIM)
        o_ref[:, hs] = (acc_ref[h] / l_ref[h]).T.astype(BF16)


def _attn_prompt(qb, kb, vt, c, ct, *, batch, seq, tq=ATTN_BLOCK):
    nblk = seq // tq
    return pl.pallas_call(
        functools.partial(_attn_prompt_kernel, tq=tq, nblk=nblk, scale=HEAD_DIM ** -0.5),
        grid=(batch, nblk),
        in_specs=[
            pl.BlockSpec((tq, ATTN_WIDTH), lambda b, i: (b * nblk + i, 0)),
            pl.BlockSpec((seq, ATTN_WIDTH), lambda b, i: (b, 0)),
            pl.BlockSpec((nblk, ATTN_WIDTH, tq), lambda b, i: (b, 0, 0)),
            pl.BlockSpec((1, seq, N_HEADS), lambda b, i: (b, 0, 0)),
            pl.BlockSpec((1, N_HEADS * nblk, tq), lambda b, i: (b, 0, 0)),
        ],
        out_specs=pl.BlockSpec((tq, ATTN_WIDTH), lambda b, i: (b * nblk + i, 0)),
        out_shape=jax.ShapeDtypeStruct((batch * seq, ATTN_WIDTH), BF16),
        scratch_shapes=[pltpu.VMEM((N_HEADS, 1, tq), F32), pltpu.VMEM((N_HEADS, 1, tq), F32),
                        pltpu.VMEM((N_HEADS, HEAD_DIM, tq), F32), pltpu.VMEM((N_HEADS, 1, tq), F32),
                        pltpu.VMEM((tq, N_HEADS), F32), pltpu.VMEM((N_HEADS, tq, tq), F32)],
        compiler_params=_params(("parallel", "arbitrary")),
        name="attn_prompt",
    )(qb, kb, vt, c, ct.reshape(batch, N_HEADS * nblk, tq))


def _attn_sample_kernel(q_ref, kp_ref, vp_ref, kn_ref, vn_ref, cq_ref, ctp_ref, ctn_ref, o_ref, *, tq, past, scale):
    row = lax.broadcasted_iota(jnp.int32, (tq, tq), 0)
    col = lax.broadcasted_iota(jnp.int32, (tq, tq), 1)
    diag_mask = col <= row
    for h in range(N_HEADS):
        hs = slice(h * HEAD_DIM, (h + 1) * HEAD_DIM)
        q = q_ref[:, hs]
        cq = cq_ref[0, :, h:h + 1]
        carry = _softmax_init(tq)
        head_rows = pl.ds(h, past, stride=N_HEADS)
        carry = _softmax_step(carry, q, kp_ref[0, head_rows, :].astype(BF16), vp_ref[0, head_rows, :].astype(BF16),
                              cq, ctp_ref[0, h:h + 1, :], None, scale)
        _, l, acc = _softmax_step(carry, q, kn_ref[:, hs], vn_ref[:, hs],
                                  cq, ctn_ref[0, h:h + 1, :], diag_mask, scale)
        o_ref[:, hs] = (acc / l).astype(BF16)


def _attn_sample(qb, kb, vb, k_cache, v_cache, cq, ct_past, ct_new, *, layer, batch, tq, past):
    k_cache = k_cache.reshape(-1, past * N_HEADS, HEAD_DIM)
    v_cache = v_cache.reshape(-1, past * N_HEADS, HEAD_DIM)
    new = lambda: pl.BlockSpec((tq, ATTN_WIDTH), lambda b: (b, 0))
    old = lambda: pl.BlockSpec((1, past * N_HEADS, HEAD_DIM), lambda b: (layer * batch + b, 0, 0))
    return pl.pallas_call(
        functools.partial(_attn_sample_kernel, tq=tq, past=past, scale=HEAD_DIM ** -0.5),
        grid=(batch,),
        in_specs=[
            new(), old(), old(), new(), new(),
            pl.BlockSpec((1, tq, N_HEADS), lambda b: (b, 0, 0)),
            pl.BlockSpec((1, N_HEADS, past), lambda b: (b, 0, 0)),
            pl.BlockSpec((1, N_HEADS, tq), lambda b: (b, 0, 0)),
        ],
        out_specs=new(),
        out_shape=jax.ShapeDtypeStruct((batch * tq, ATTN_WIDTH), BF16),
        compiler_params=_params(("parallel",)),
        name="attn_sample",
    )(qb, k_cache, v_cache, kb, vb, cq, ct_past, ct_new)


def _pool_kernel(u_ref, prev_ref, d_ref, b0_ref, b1_ref, *, pos0):
    ts = u_ref.shape[1]
    pos = pos0 + lax.broadcasted_iota(jnp.int32, (ts, 1), 0)
    for g, d in enumerate(_pool_rows(prev_ref[0], u_ref[0], pos, b0_ref, b1_ref)):
        d_ref[0, :, g * POOL_GROUP:(g + 1) * POOL_GROUP] = d


def _pool(u, prev, *, pos0):
    b, n, c = u.shape
    return pl.pallas_call(
        functools.partial(_pool_kernel, pos0=pos0),
        grid=(b,),
        in_specs=[
            pl.BlockSpec((1, n, c), lambda i: (i, 0, 0)),
            pl.BlockSpec((1, HIST_ROWS, c), lambda i: (i, 0, 0)),
        ],
        out_specs=pl.BlockSpec((1, n, c), lambda i: (i, 0, 0)),
        out_shape=jax.ShapeDtypeStruct((b, n, c), BF16),
        scratch_shapes=[pltpu.VMEM((POOL_PAD + HIST_ROWS + n, c), F32)] * 2,
        compiler_params=_params(("parallel",)),
        name="pool",
    )(u, prev)


def _merge_kernel(x_ref, h_ref, o_ref, d_ref, wga_ref, wgb_ref, wat_ref, wpl_ref, ps_ref, wo_ref, y_ref):
    h = h_ref[...]
    o = o_ref[...]
    y = x_ref[...]
    for n in range(len(POOL_WINDOWS)):
        cols = slice(n * POOL_OUT_GROUP, (n + 1) * POOL_OUT_GROUP)
        gate_a = jax.nn.sigmoid(jnp.dot(h, wga_ref[:, cols], preferred_element_type=F32))
        gate_b = jax.nn.sigmoid(jnp.dot(h, wgb_ref[:, cols], preferred_element_type=F32))
        attn = jnp.dot(o, wat_ref[:, cols], preferred_element_type=F32)
        pool = jnp.dot(d_ref[:, n * POOL_GROUP:(n + 1) * POOL_GROUP], wpl_ref[n],
                       preferred_element_type=F32) * ps_ref[:, cols]
        mixed = (gate_a * attn + gate_b * pool).astype(BF16)
        y = y + jnp.dot(mixed, wo_ref[cols, :], preferred_element_type=F32)
    y_ref[...] = y


def _merge(x, h, o, d, wga, wgb, wat, wpl, ps, wo, *, tm=256):
    t, dm = x.shape
    row = lambda width: pl.BlockSpec((tm, width), lambda i: (i, 0))
    return pl.pallas_call(
        _merge_kernel,
        grid=(t // tm,),
        in_specs=[row(dm), row(dm), row(ATTN_WIDTH), row(POOL_WIDTH),
                  _resident(wga.shape), _resident(wgb.shape), _resident(wat.shape), _resident(wpl.shape),
                  _resident(ps.shape), _resident(wo.shape)],
        out_specs=row(dm),
        out_shape=jax.ShapeDtypeStruct((t, dm), F32),
        compiler_params=_params(("parallel",)),
        name="merge",
    )(x, h, o, d, wga, wgb, wat, wpl, ps, wo)


def _layer_weights(l, ffn1_norm, ffn1_w_gate, ffn1_w_up, ffn1_w_down, mix_norm, w_in, b_forget, w_branch_attn,
                   w_pool_group, pool_scale, w_out, ffn2_norm, ffn2_w_gate, ffn2_w_up, ffn2_w_down):
    a = ATTN_WIDTH
    wi = w_in[l]
    f0, u0 = 3 * a, 3 * a + N_HEADS
    ga0 = u0 + POOL_WIDTH
    gb0 = ga0 + D_MODEL
    wf = jnp.pad(wi[:, f0:u0], ((0, 0), (0, LANES - N_HEADS)))
    return dict(
        n1=ffn1_norm[l][None], g1=ffn1_w_gate[l].astype(BF16), u1=ffn1_w_up[l].astype(BF16),
        d1=(0.5 * ffn1_w_down[l]).astype(BF16),
        nm=mix_norm[l][None],
        wq=wi[:, :a].astype(BF16), wk=wi[:, a:2 * a].astype(BF16), wv=wi[:, 2 * a:3 * a].astype(BF16),
        wf=wf.astype(BF16), wu=wi[:, u0:ga0].astype(BF16),
        wga=wi[:, ga0:gb0].astype(BF16), wgb=wi[:, gb0:].astype(BF16),
        bf=b_forget[l][None],
        wat=w_branch_attn[l].astype(BF16), wpl=w_pool_group[l].astype(BF16), ps=pool_scale[l][None],
        wo=w_out[l].astype(BF16),
        n2=ffn2_norm[l][None], g2=ffn2_w_gate[l].astype(BF16), u2=ffn2_w_up[l].astype(BF16),
        d2=(0.5 * ffn2_w_down[l]).astype(BF16),
    )


def _layer(x, w, final_norm, *, batch, seq, past, last):
    x1 = _ffn(x, w["n1"], w["g1"], w["u1"], w["d1"], w["n1"], final_norm=False)
    proj = _inproj(x1, w["nm"], w["wq"], w["wk"], w["wv"], w["wu"], w["wf"], w["bf"],
                   seq=seq if past is None else None)
    qb, k, v, kb, vb, u, logf, hn = proj[:8]
    if past is None:
        d, c, ct = proj[8:]
        o = _attn_prompt(qb, kb, vb, c.reshape(batch, seq, N_HEADS), ct, batch=batch, seq=seq)
        logf_t = logf
        new_pool = u[:, HIST_ROWS - POOL_HIST:, :]
    else:
        layer, k_cache, v_cache, logf_cache, hist = past
        u3 = u.reshape(batch, seq, POOL_WIDTH)
        plen = k_cache.shape[2]
        logf_cache_t = jnp.swapaxes(logf_cache, 2, 3).reshape(-1, N_HEADS, plen)
        ct_past, ct_new, cq, logf_t = _cumsum_rows(logf_cache_t, logf.reshape(batch, seq, N_HEADS), layer=layer)
        o = _attn_sample(qb, kb, vb, k_cache, v_cache, cq, ct_past, ct_new, layer=layer, batch=batch, tq=seq,
                         past=plen)
        hist_rows = jnp.pad(hist, ((0, 0), (HIST_ROWS - POOL_HIST, 0), (0, 0)))
        d = _pool(u3, hist_rows, pos0=plen).reshape(batch * seq, POOL_WIDTH)
        new_pool = jnp.concatenate([hist, u3], axis=1)[:, -POOL_HIST:, :]
    x2 = _merge(x1, hn, o, d, w["wga"], w["wgb"], w["wat"], w["wpl"], w["ps"], w["wo"])
    x3 = _ffn(x2, w["n2"], w["g2"], w["u2"], w["d2"], final_norm, final_norm=last)
    return (x3, k.reshape(batch, seq, N_HEADS, HEAD_DIM), v.reshape(batch, seq, N_HEADS, HEAD_DIM),
            jnp.swapaxes(logf_t, 1, 2), new_pool)


def kernel(x_prompt, x_sample, cache_k, cache_v, cache_logf, state_pool, ffn1_norm, ffn1_w_gate, ffn1_w_up,
           ffn1_w_down, mix_norm, w_in, b_forget, w_branch_attn, w_pool_group, pool_scale, w_out, ffn2_norm,
           ffn2_w_gate, ffn2_w_up, ffn2_w_down, final_norm):
    depth = w_in.shape[0]
    bp, sp, _ = x_prompt.shape
    bs, ss, _ = x_sample.shape
    hp = x_prompt.reshape(bp * sp, D_MODEL)
    hs = x_sample.reshape(bs * ss, D_MODEL)
    fn = final_norm[None]
    outs = [[] for _ in range(8)]
    for l in range(depth):
        w = _layer_weights(l, ffn1_norm, ffn1_w_gate, ffn1_w_up, ffn1_w_down, mix_norm, w_in, b_forget,
                           w_branch_attn, w_pool_group, pool_scale, w_out, ffn2_norm, ffn2_w_gate, ffn2_w_up,
                           ffn2_w_down)
        last = l == depth - 1
        hp, k1, v1, l1, p1 = _layer(hp, w, fn, batch=bp, seq=sp, past=None, last=last)
        hs, k2, v2, l2, p2 = _layer(hs, w, fn, batch=bs, seq=ss,
                                    past=(l, cache_k, cache_v, cache_logf, state_pool[l]), last=last)
        for dst, val in zip(outs, (k1, v1, l1, p1, k2, v2, l2, p2)):
            dst.append(val)
    stacked = [jnp.stack(o) for o in outs]
    return (hp.reshape(bp, sp, D_MODEL), hs.reshape(bs, ss, D_MODEL), *stacked)
```

```python
import functools

import jax
import jax.numpy as jnp
from jax import lax
from jax.experimental import pallas as pl
from jax.experimental.pallas import tpu as pltpu

D_MODEL = 2048
N_HEADS = 8
HEAD_DIM = 128
ATTN_WIDTH = N_HEADS * HEAD_DIM
POOL_WINDOWS = (2, 4, 8, 16)
POOL_WIDTH = D_MODEL // 2
POOL_GROUP = POOL_WIDTH // len(POOL_WINDOWS)
POOL_HIST = max(POOL_WINDOWS) - 1
RMS_EPS = 1e-6
NEG_INF = -1e30
LOG2E = 1.4426950408889634

LANES = 128
HIST_ROWS = 16
POOL_PAD = 8
ATTN_BLOCK = 256
VMEM_LIMIT = 60 * 1024 * 1024

F32 = jnp.float32
BF16 = jnp.bfloat16


def _params(semantics):
    return pltpu.CompilerParams(dimension_semantics=semantics, vmem_limit_bytes=VMEM_LIMIT)


def _rmsnorm(x, g):
    r = lax.rsqrt(jnp.mean(x * x, axis=-1, keepdims=True) + RMS_EPS)
    return (x * r) * g


def _resident(shape):
    return pl.BlockSpec(shape, lambda *_: (0,) * len(shape), pipeline_mode=pl.Buffered(1))


def _ffn_kernel(x_ref, g_ref, wg_ref, wu_ref, wd_ref, fn_ref, o_ref, h_ref, *, final_norm):
    j = pl.program_id(1)

    @pl.when(j == 0)
    def _():
        x = x_ref[...]
        h_ref[...] = _rmsnorm(x, g_ref[...]).astype(BF16)
        o_ref[...] = x

    h = h_ref[...]
    gate = jnp.dot(h, wg_ref[...], preferred_element_type=F32)
    up = jnp.dot(h, wu_ref[...], preferred_element_type=F32)
    act = (gate * jax.nn.sigmoid(gate) * up).astype(BF16)
    o_ref[...] += jnp.dot(act, wd_ref[...], preferred_element_type=F32)

    if final_norm:
        @pl.when(j == pl.num_programs(1) - 1)
        def _():
            o_ref[...] = _rmsnorm(o_ref[...], fn_ref[...])


def _ffn(x, g, wg, wu, wd, fn, *, final_norm, tm=1024, tf=512):
    t, d = x.shape
    f = wg.shape[1]
    return pl.pallas_call(
        functools.partial(_ffn_kernel, final_norm=final_norm),
        grid=(t // tm, f // tf),
        in_specs=[
            pl.BlockSpec((tm, d), lambda i, j: (i, 0)),
            pl.BlockSpec((1, d), lambda i, j: (0, 0)),
            pl.BlockSpec((d, tf), lambda i, j: (0, j)),
            pl.BlockSpec((d, tf), lambda i, j: (0, j)),
            pl.BlockSpec((tf, d), lambda i, j: (j, 0)),
            pl.BlockSpec((1, d), lambda i, j: (0, 0)),
        ],
        out_specs=pl.BlockSpec((tm, d), lambda i, j: (i, 0)),
        out_shape=jax.ShapeDtypeStruct((t, d), F32),
        scratch_shapes=[pltpu.VMEM((tm, d), BF16)],
        compiler_params=_params(("parallel", "arbitrary")),
        name="ffn",
    )(x, g, wg, wu, wd, fn)


def _running_sum(x, buf_ref, pad):
    n = x.shape[0]
    shift = 1
    while shift < n:
        buf_ref[pl.ds(pad, n), :] = x
        x = x + buf_ref[pl.ds(pad - shift, n), :]
        shift *= 2
    return x


def _largest_shift(n):
    shift = 1
    while shift * 2 < n:
        shift *= 2
    return max(shift, 8)


def _pool_rows(prev, cur, pos, b0_ref, b1_ref):
    ts = cur.shape[0]
    n = HIST_ROWS + ts
    zero_rows = jnp.zeros((POOL_PAD, cur.shape[1]), F32)
    b0_ref[pl.ds(0, POOL_PAD), :] = zero_rows
    b1_ref[pl.ds(0, POOL_PAD), :] = zero_rows
    b0_ref[pl.ds(POOL_PAD, HIST_ROWS), :] = prev
    b0_ref[pl.ds(POOL_PAD + HIST_ROWS, ts), :] = cur
    out = []
    for g, w in enumerate(POOL_WINDOWS):
        gs = slice(g * POOL_GROUP, (g + 1) * POOL_GROUP)
        src, dst = b0_ref, b1_ref
        shift = 1
        while True:
            win = src[pl.ds(POOL_PAD, n), gs] + src[pl.ds(POOL_PAD - shift, n), gs]
            shift *= 2
            if shift == w:
                break
            dst[pl.ds(POOL_PAD, n), gs] = win
            src, dst = dst, src
        cnt = jnp.minimum(w, pos + 1).astype(F32)
        out.append((win[HIST_ROWS:] / cnt - cur[:, gs]).astype(BF16))
    return out


def _log_sigmoid(x):
    return jnp.minimum(x, 0.0) - jnp.log1p(jnp.exp(-jnp.abs(x)))


def _inproj_kernel(x_ref, g_ref, wq_ref, wk_ref, wv_ref, wu_ref, wf_ref, bf_ref,
                   qb_ref, k_ref, v_ref, kb_ref, vb_ref, u_ref, logf_ref, hn_ref, *rest, tiles_per_seq, csum_pad):
    if tiles_per_seq:
        d_ref, c_ref, ct_ref, hist_ref, total_ref, b0_ref, b1_ref, cbuf_ref = rest
        tile = pl.program_id(0) % tiles_per_seq

        @pl.when(tile == 0)
        def _():
            hist_ref[...] = jnp.zeros_like(hist_ref)
            total_ref[...] = jnp.zeros_like(total_ref)

    h = _rmsnorm(x_ref[...], g_ref[...]).astype(BF16)
    hn_ref[...] = h
    u = jnp.dot(h, wu_ref[...], preferred_element_type=F32)
    if tiles_per_seq:
        u_ref[0] = u[u.shape[0] - HIST_ROWS:, :]
    else:
        u_ref[...] = u
    f = jnp.dot(h, wf_ref[...], preferred_element_type=F32)
    logf = _log_sigmoid(f[:, :N_HEADS] + bf_ref[...])
    if tiles_per_seq:
        logf_ref[0] = logf.T
        tm = u.shape[0]
        pos = tile * tm + lax.broadcasted_iota(jnp.int32, (tm, 1), 0)
        for g, d in enumerate(_pool_rows(hist_ref[...], u, pos, b0_ref, b1_ref)):
            d_ref[:, g * POOL_GROUP:(g + 1) * POOL_GROUP] = d
        hist_ref[...] = u[tm - HIST_ROWS:, :]
        cbuf_ref[pl.ds(0, csum_pad), :] = jnp.zeros((csum_pad, N_HEADS), F32)
        c = _running_sum(logf, cbuf_ref, csum_pad) + total_ref[...]
        c_ref[...] = c
        ct_ref[0] = c.T
        total_ref[...] = c[tm - 1:, :]
    else:
        logf_ref[...] = logf

    qb_ref[...] = jnp.dot(h, wq_ref[...], preferred_element_type=F32).astype(BF16)
    k = jnp.dot(h, wk_ref[...], preferred_element_type=F32)
    k_ref[...] = k
    kb_ref[...] = k.astype(BF16)
    v = jnp.dot(h, wv_ref[...], preferred_element_type=F32)
    v_ref[...] = v
    if tiles_per_seq:
        vb_ref[0] = v.T.astype(BF16)
    else:
        vb_ref[...] = v.astype(BF16)


def _inproj(x, g, wq, wk, wv, wu, wf, bforget, *, seq, tm=ATTN_BLOCK):
    t, d = x.shape
    assert t % tm == 0 and (seq is None or seq % tm == 0), (t, seq, tm)
    row = lambda width: pl.BlockSpec((tm, width), lambda i: (i, 0))
    out_specs = [row(ATTN_WIDTH), row(ATTN_WIDTH), row(ATTN_WIDTH), row(ATTN_WIDTH), row(ATTN_WIDTH),
                 row(POOL_WIDTH), row(N_HEADS), row(d)]
    out_shape = [
        jax.ShapeDtypeStruct((t, ATTN_WIDTH), BF16),
        jax.ShapeDtypeStruct((t, ATTN_WIDTH), F32),
        jax.ShapeDtypeStruct((t, ATTN_WIDTH), F32),
        jax.ShapeDtypeStruct((t, ATTN_WIDTH), BF16),
        jax.ShapeDtypeStruct((t, ATTN_WIDTH), BF16),
        jax.ShapeDtypeStruct((t, POOL_WIDTH), F32),
        jax.ShapeDtypeStruct((t, N_HEADS), F32),
        jax.ShapeDtypeStruct((t, d), BF16),
    ]
    scratch = []
    tiles_per_seq, csum_pad = 0, 0
    if seq is not None:
        tiles_per_seq, csum_pad = seq // tm, _largest_shift(tm)
        time_minor = pl.BlockSpec((1, N_HEADS, tm), lambda i: (i // tiles_per_seq, 0, i % tiles_per_seq))
        out_specs[4] = pl.BlockSpec((1, ATTN_WIDTH, tm), lambda i: (i, 0, 0))
        out_shape[4] = jax.ShapeDtypeStruct((t // tm, ATTN_WIDTH, tm), BF16)
        out_specs[5] = pl.BlockSpec((1, HIST_ROWS, POOL_WIDTH), lambda i: (i // tiles_per_seq, 0, 0))
        out_shape[5] = jax.ShapeDtypeStruct((t // seq, HIST_ROWS, POOL_WIDTH), F32)
        out_specs[6] = time_minor
        out_shape[6] = jax.ShapeDtypeStruct((t // seq, N_HEADS, seq), F32)
        out_specs += [row(POOL_WIDTH), row(N_HEADS), time_minor]
        out_shape += [jax.ShapeDtypeStruct((t, POOL_WIDTH), BF16), jax.ShapeDtypeStruct((t, N_HEADS), F32),
                      jax.ShapeDtypeStruct((t // seq, N_HEADS, seq), F32)]
        scratch = [pltpu.VMEM((HIST_ROWS, POOL_WIDTH), F32), pltpu.VMEM((1, N_HEADS), F32),
                   pltpu.VMEM((POOL_PAD + HIST_ROWS + tm, POOL_WIDTH), F32),
                   pltpu.VMEM((POOL_PAD + HIST_ROWS + tm, POOL_WIDTH), F32),
                   pltpu.VMEM((csum_pad + tm, N_HEADS), F32)]
    return pl.pallas_call(
        functools.partial(_inproj_kernel, tiles_per_seq=tiles_per_seq, csum_pad=csum_pad),
        grid=(t // tm,),
        in_specs=[
            row(d),
            _resident((1, d)),
            _resident((d, ATTN_WIDTH)),
            _resident((d, ATTN_WIDTH)),
            _resident((d, ATTN_WIDTH)),
            _resident((d, POOL_WIDTH)),
            _resident((d, LANES)),
            _resident((1, N_HEADS)),
        ],
        out_specs=out_specs,
        out_shape=out_shape,
        scratch_shapes=scratch,
        compiler_params=_params(("arbitrary",)),
        name="inproj",
    )(x, g, wq, wk, wv, wu, wf, bforget)


def _cumsum_kernel(past_ref, new_ref, ctp_ref, ctn_ref, cq_ref, lt_ref, buf_ref, *, pad):
    p = past_ref.shape[2]
    new = new_ref[0]
    buf_ref[pl.ds(0, pad), :] = jnp.zeros((pad, new.shape[1]), F32)
    rows = jnp.concatenate([past_ref[0].astype(F32).T, new], axis=0)
    c = _running_sum(rows, buf_ref, pad)
    ctp_ref[0] = c[:p].T
    ctn_ref[0] = c[p:].T
    cq_ref[0] = c[p:]
    lt_ref[0] = new.T


def _cumsum_rows(past, new, *, layer):
    b, t, h = new.shape
    p = past.shape[2]
    n = p + t
    pad = _largest_shift(n)
    per_stream = lambda *shape: pl.BlockSpec((1,) + shape, lambda i: (i, 0, 0))
    return pl.pallas_call(
        functools.partial(_cumsum_kernel, pad=pad),
        grid=(b,),
        in_specs=[pl.BlockSpec((1, h, p), lambda i: (layer * b + i, 0, 0)), per_stream(t, h)],
        out_specs=[per_stream(h, p), per_stream(h, t), per_stream(t, h), per_stream(h, t)],
        out_shape=[jax.ShapeDtypeStruct((b, h, p), F32), jax.ShapeDtypeStruct((b, h, t), F32),
                   jax.ShapeDtypeStruct((b, t, h), F32), jax.ShapeDtypeStruct((b, h, t), F32)],
        scratch_shapes=[pltpu.VMEM((pad + n, h), F32)],
        compiler_params=_params(("parallel",)),
        name="cumsum",
    )(past, new)


def _softmax_step(carry, q, k, v, cq, ck, mask, scale):
    m, l, acc = carry
    s = lax.dot_general(q, k, (((1,), (1,)), ((), ())), preferred_element_type=F32) * scale
    logit = s + (cq - ck)
    if mask is not None:
        logit = jnp.where(mask, logit, NEG_INF)
    m_new = jnp.maximum(m, jnp.max(logit, axis=1, keepdims=True))
    alpha = jnp.exp(m - m_new)
    p = jnp.exp(logit - m_new)
    l = alpha * l + jnp.sum(p, axis=1, keepdims=True)
    acc = alpha * acc + jnp.dot(p.astype(BF16), v, preferred_element_type=F32)
    return m_new, l, acc


def _softmax_init(tq):
    return (jnp.full((tq, 1), NEG_INF, F32), jnp.zeros((tq, 1), F32), jnp.zeros((tq, HEAD_DIM), F32))


def _attn_prompt_kernel(q_ref, k_ref, vt_ref, c_ref, ct_ref, o_ref, m_ref, l_ref, acc_ref, cq_ref,
                        cs_ref, a_ref, *, tq, nblk, scale):
    qi = pl.program_id(1)
    scale2 = scale * LOG2E
    key = lax.broadcasted_iota(jnp.int32, (tq, tq), 0)
    qry = lax.broadcasted_iota(jnp.int32, (tq, tq), 1)
    diag_mask = key <= qry
    for h in range(N_HEADS):
        m_ref[h] = jnp.full((1, tq), NEG_INF, F32)
        l_ref[h] = jnp.zeros((1, tq), F32)
        acc_ref[h] = jnp.zeros((HEAD_DIM, tq), F32)
        cq_ref[h] = ct_ref[0, pl.ds(h * nblk + qi, 1), :] * LOG2E

    def block(kb, mask):
        k0 = pl.multiple_of(kb * tq, tq)
        cs_ref[...] = c_ref[0, pl.ds(k0, tq), :] * LOG2E
        col_max = []
        for h in range(N_HEADS):
            hs = slice(h * HEAD_DIM, (h + 1) * HEAD_DIM)
            st = lax.dot_general(k_ref[pl.ds(k0, tq), hs], q_ref[:, hs], (((1,), (1,)), ((), ())),
                                 preferred_element_type=F32)
            a = st * scale2 - cs_ref[:, h:h + 1]
            if mask is not None:
                a = jnp.where(mask, a, NEG_INF)
            a_ref[h] = a
            col_max.append(jnp.max(a, axis=0, keepdims=True))
        for h in range(N_HEADS):
            hs = slice(h * HEAD_DIM, (h + 1) * HEAD_DIM)
            m = m_ref[h]
            m_new = jnp.maximum(m, col_max[h] + cq_ref[h])
            alpha = jnp.exp2(m - m_new)
            p = jnp.exp2(a_ref[h] - (m_new - cq_ref[h]))
            l_ref[h] = alpha * l_ref[h] + jnp.sum(p, axis=0, keepdims=True)
            acc_ref[h] = alpha * acc_ref[h] + jnp.dot(vt_ref[kb, hs, :], p.astype(BF16),
                                                      preferred_element_type=F32)
            m_ref[h] = m_new

    def body(kb, carry):
        block(kb, None)
        return carry

    lax.fori_loop(0, qi, body, 0)
    block(qi, diag_mask)
    for h in range(N_HEADS):
        hs = slice(h * HEAD_DIM, (h + 1) * HEAD_DIM)
        o_ref[:, hs] = (acc_ref[h] / l_ref[h]).T.astype(BF16)


def _attn_prompt(qb, kb, vt, c, ct, *, batch, seq, tq=ATTN_BLOCK):
    nblk = seq // tq
    return pl.pallas_call(
        functools.partial(_attn_prompt_kernel, tq=tq, nblk=nblk, scale=HEAD_DIM ** -0.5),
        grid=(batch, nblk),
        in_specs=[
            pl.BlockSpec((tq, ATTN_WIDTH), lambda b, i: (b * nblk + i, 0)),
            pl.BlockSpec((seq, ATTN_WIDTH), lambda b, i: (b, 0)),
            pl.BlockSpec((nblk, ATTN_WIDTH, tq), lambda b, i: (b, 0, 0)),
            pl.BlockSpec((1, seq, N_HEADS), lambda b, i: (b, 0, 0)),
            pl.BlockSpec((1, N_HEADS * nblk, tq), lambda b, i: (b, 0, 0)),
        ],
        out_specs=pl.BlockSpec((tq, ATTN_WIDTH), lambda b, i: (b * nblk + i, 0)),
        out_shape=jax.ShapeDtypeStruct((batch * seq, ATTN_WIDTH), BF16),
        scratch_shapes=[pltpu.VMEM((N_HEADS, 1, tq), F32), pltpu.VMEM((N_HEADS, 1, tq), F32),
                        pltpu.VMEM((N_HEADS, HEAD_DIM, tq), F32), pltpu.VMEM((N_HEADS, 1, tq), F32),
                        pltpu.VMEM((tq, N_HEADS), F32), pltpu.VMEM((N_HEADS, tq, tq), F32)],
        compiler_params=_params(("parallel", "arbitrary")),
        name="attn_prompt",
    )(qb, kb, vt, c, ct.reshape(batch, N_HEADS * nblk, tq))


def _attn_sample_kernel(q_ref, kp_ref, vp_ref, kn_ref, vn_ref, cq_ref, ctp_ref, ctn_ref, o_ref, *, tq, past, scale):
    row = lax.broadcasted_iota(jnp.int32, (tq, tq), 0)
    col = lax.broadcasted_iota(jnp.int32, (tq, tq), 1)
    diag_mask = col <= row
    for h in range(N_HEADS):
        hs = slice(h * HEAD_DIM, (h + 1) * HEAD_DIM)
        q = q_ref[:, hs]
        cq = cq_ref[0, :, h:h + 1]
        carry = _softmax_init(tq)
        head_rows = pl.ds(h, past, stride=N_HEADS)
        carry = _softmax_step(carry, q, kp_ref[0, head_rows, :].astype(BF16), vp_ref[0, head_rows, :].astype(BF16),
                              cq, ctp_ref[0, h:h + 1, :], None, scale)
        _, l, acc = _softmax_step(carry, q, kn_ref[:, hs], vn_ref[:, hs],
                                  cq, ctn_ref[0, h:h + 1, :], diag_mask, scale)
        o_ref[:, hs] = (acc / l).astype(BF16)


def _attn_sample(qb, kb, vb, k_cache, v_cache, cq, ct_past, ct_new, *, layer, batch, tq, past):
    k_cache = k_cache.reshape(-1, past * N_HEADS, HEAD_DIM)
    v_cache = v_cache.reshape(-1, past * N_HEADS, HEAD_DIM)
    new = lambda: pl.BlockSpec((tq, ATTN_WIDTH), lambda b: (b, 0))
    old = lambda: pl.BlockSpec((1, past * N_HEADS, HEAD_DIM), lambda b: (layer * batch + b, 0, 0))
    return pl.pallas_call(
        functools.partial(_attn_sample_kernel, tq=tq, past=past, scale=HEAD_DIM ** -0.5),
        grid=(batch,),
        in_specs=[
            new(), old(), old(), new(), new(),
            pl.BlockSpec((1, tq, N_HEADS), lambda b: (b, 0, 0)),
            pl.BlockSpec((1, N_HEADS, past), lambda b: (b, 0, 0)),
            pl.BlockSpec((1, N_HEADS, tq), lambda b: (b, 0, 0)),
        ],
        out_specs=new(),
        out_shape=jax.ShapeDtypeStruct((batch * tq, ATTN_WIDTH), BF16),
        compiler_params=_params(("parallel",)),
        name="attn_sample",
    )(qb, k_cache, v_cache, kb, vb, cq, ct_past, ct_new)


def _pool_kernel(u_ref, prev_ref, d_ref, b0_ref, b1_ref, *, pos0):
    ts = u_ref.shape[1]
    pos = pos0 + lax.broadcasted_iota(jnp.int32, (ts, 1), 0)
    for g, d in enumerate(_pool_rows(prev_ref[0], u_ref[0], pos, b0_ref, b1_ref)):
        d_ref[0, :, g * POOL_GROUP:(g + 1) * POOL_GROUP] = d


def _pool(u, prev, *, pos0):
    b, n, c = u.shape
    return pl.pallas_call(
        functools.partial(_pool_kernel, pos0=pos0),
        grid=(b,),
        in_specs=[
            pl.BlockSpec((1, n, c), lambda i: (i, 0, 0)),
            pl.BlockSpec((1, HIST_ROWS, c), lambda i: (i, 0, 0)),
        ],
        out_specs=pl.BlockSpec((1, n, c), lambda i: (i, 0, 0)),
        out_shape=jax.ShapeDtypeStruct((b, n, c), BF16),
        scratch_shapes=[pltpu.VMEM((POOL_PAD + HIST_ROWS + n, c), F32)] * 2,
        compiler_params=_params(("parallel",)),
        name="pool",
    )(u, prev)


def _merge_kernel(x_ref, h_ref, o_ref, d_ref, wga_ref, wgb_ref, wat_ref, wpl_ref, ps_ref, wo_ref, y_ref):
    h = h_ref[...]
    o = o_ref[...]
    y = x_ref[...]
    groups = len(POOL_WINDOWS)
    gate_a = jax.nn.sigmoid(jnp.dot(h, wga_ref[...], preferred_element_type=F32))
    gate_b = jax.nn.sigmoid(jnp.dot(h, wgb_ref[...], preferred_element_type=F32))
    attn = jnp.dot(o, wat_ref[...], preferred_element_type=F32)
    pool = jnp.concatenate(
        [jnp.dot(d_ref[:, g * POOL_GROUP:(g + 1) * POOL_GROUP], wpl_ref[g], preferred_element_type=F32)
         for g in range(groups)], axis=1) * ps_ref[...]
    mixed = (gate_a * attn + gate_b * pool).astype(BF16)
    y_ref[...] = y + jnp.dot(mixed, wo_ref[...], preferred_element_type=F32)


def _merge(x, h, o, d, wga, wgb, wat, wpl, ps, wo, *, tm=256):
    t, dm = x.shape
    row = lambda width: pl.BlockSpec((tm, width), lambda i: (i, 0))
    return pl.pallas_call(
        _merge_kernel,
        grid=(t // tm,),
        in_specs=[row(dm), row(dm), row(ATTN_WIDTH), row(POOL_WIDTH),
                  _resident(wga.shape), _resident(wgb.shape), _resident(wat.shape), _resident(wpl.shape),
                  _resident(ps.shape), _resident(wo.shape)],
        out_specs=row(dm),
        out_shape=jax.ShapeDtypeStruct((t, dm), F32),
        compiler_params=_params(("parallel",)),
        name="merge",
    )(x, h, o, d, wga, wgb, wat, wpl, ps, wo)


def _layer_weights(l, ffn1_norm, ffn1_w_gate, ffn1_w_up, ffn1_w_down, mix_norm, w_in, b_forget, w_branch_attn,
                   w_pool_group, pool_scale, w_out, ffn2_norm, ffn2_w_gate, ffn2_w_up, ffn2_w_down):
    a = ATTN_WIDTH
    wi = w_in[l]
    f0, u0 = 3 * a, 3 * a + N_HEADS
    ga0 = u0 + POOL_WIDTH
    gb0 = ga0 + D_MODEL
    wf = jnp.pad(wi[:, f0:u0], ((0, 0), (0, LANES - N_HEADS)))
    return dict(
        n1=ffn1_norm[l][None], g1=ffn1_w_gate[l].astype(BF16), u1=ffn1_w_up[l].astype(BF16),
        d1=(0.5 * ffn1_w_down[l]).astype(BF16),
        nm=mix_norm[l][None],
        wq=wi[:, :a].astype(BF16), wk=wi[:, a:2 * a].astype(BF16), wv=wi[:, 2 * a:3 * a].astype(BF16),
        wf=wf.astype(BF16), wu=wi[:, u0:ga0].astype(BF16),
        wga=wi[:, ga0:gb0].astype(BF16), wgb=wi[:, gb0:].astype(BF16),
        bf=b_forget[l][None],
        wat=w_branch_attn[l].astype(BF16), wpl=w_pool_group[l].astype(BF16), ps=pool_scale[l][None],
        wo=w_out[l].astype(BF16),
        n2=ffn2_norm[l][None], g2=ffn2_w_gate[l].astype(BF16), u2=ffn2_w_up[l].astype(BF16),
        d2=(0.5 * ffn2_w_down[l]).astype(BF16),
    )


def _layer(x, w, final_norm, *, batch, seq, past, last):
    x1 = _ffn(x, w["n1"], w["g1"], w["u1"], w["d1"], w["n1"], final_norm=False)
    proj = _inproj(x1, w["nm"], w["wq"], w["wk"], w["wv"], w["wu"], w["wf"], w["bf"],
                   seq=seq if past is None else None)
    qb, k, v, kb, vb, u, logf, hn = proj[:8]
    if past is None:
        d, c, ct = proj[8:]
        o = _attn_prompt(qb, kb, vb, c.reshape(batch, seq, N_HEADS), ct, batch=batch, seq=seq)
        logf_t = logf
        new_pool = u[:, HIST_ROWS - POOL_HIST:, :]
    else:
        layer, k_cache, v_cache, logf_cache, hist = past
        u3 = u.reshape(batch, seq, POOL_WIDTH)
        plen = k_cache.shape[2]
        logf_cache_t = jnp.swapaxes(logf_cache, 2, 3).reshape(-1, N_HEADS, plen)
        ct_past, ct_new, cq, logf_t = _cumsum_rows(logf_cache_t, logf.reshape(batch, seq, N_HEADS), layer=layer)
        o = _attn_sample(qb, kb, vb, k_cache, v_cache, cq, ct_past, ct_new, layer=layer, batch=batch, tq=seq,
                         past=plen)
        hist_rows = jnp.pad(hist, ((0, 0), (HIST_ROWS - POOL_HIST, 0), (0, 0)))
        d = _pool(u3, hist_rows, pos0=plen).reshape(batch * seq, POOL_WIDTH)
        new_pool = jnp.concatenate([hist, u3], axis=1)[:, -POOL_HIST:, :]
    x2 = _merge(x1, hn, o, d, w["wga"], w["wgb"], w["wat"], w["wpl"], w["ps"], w["wo"])
    x3 = _ffn(x2, w["n2"], w["g2"], w["u2"], w["d2"], final_norm, final_norm=last)
    return (x3, k.reshape(batch, seq, N_HEADS, HEAD_DIM), v.reshape(batch, seq, N_HEADS, HEAD_DIM),
            jnp.swapaxes(logf_t, 1, 2), new_pool)


def kernel(x_prompt, x_sample, cache_k, cache_v, cache_logf, state_pool, ffn1_norm, ffn1_w_gate, ffn1_w_up,
           ffn1_w_down, mix_norm, w_in, b_forget, w_branch_attn, w_pool_group, pool_scale, w_out, ffn2_norm,
           ffn2_w_gate, ffn2_w_up, ffn2_w_down, final_norm):
    depth = w_in.shape[0]
    bp, sp, _ = x_prompt.shape
    bs, ss, _ = x_sample.shape
    hp = x_prompt.reshape(bp * sp, D_MODEL)
    hs = x_sample.reshape(bs * ss, D_MODEL)
    fn = final_norm[None]
    outs = [[] for _ in range(8)]
    for l in range(depth):
        w = _layer_weights(l, ffn1_norm, ffn1_w_gate, ffn1_w_up, ffn1_w_down, mix_norm, w_in, b_forget,
                           w_branch_attn, w_pool_group, pool_scale, w_out, ffn2_norm, ffn2_w_gate, ffn2_w_up,
                           ffn2_w_down)
        last = l == depth - 1
        hp, k1, v1, l1, p1 = _layer(hp, w, fn, batch=bp, seq=sp, past=None, last=last)
        hs, k2, v2, l2, p2 = _layer(hs, w, fn, batch=bs, seq=ss,
                                    past=(l, cache_k, cache_v, cache_logf, state_pool[l]), last=last)
        for dst, val in zip(outs, (k1, v1, l1, p1, k2, v2, l2, p2)):
            dst.append(val)
    stacked = [jnp.stack(o) for o in outs]
    return (hp.reshape(bp, sp, D_MODEL), hs.reshape(bs, ss, D_MODEL), *stacked)
```
